```python
import math
import jax, jax.numpy as jnp
from jax import lax
import numpy as np

D_MODEL = 2048
BATCH = 16
SEQ = 2048
DEPTH = 4

N_META = 16
GRID_W = 64
Q_BLOCK = 128
ROPE_THETA = 10000.0
NORM_EPS = 1e-6
N_BRANCH = 3

C_CONV = D_MODEL // 2
CONV_K = 31

GQA_HEADS = 8
GQA_KV_HEADS = 2
GQA_HEAD_DIM = 128

MLA_HEADS = 8
MLA_Q_RANK = D_MODEL // 4
MLA_KV_RANK = D_MODEL // 4
MLA_NOPE_DIM = 128
MLA_ROPE_DIM = 64
MLA_V_DIM = 128
MLA_QK_DIM = MLA_NOPE_DIM + MLA_ROPE_DIM

D_FF = -(-8 * D_MODEL // (3 * 256)) * 256

IN_SIZES = (
    2 * C_CONV,
    GQA_HEADS * GQA_HEAD_DIM,
    GQA_KV_HEADS * GQA_HEAD_DIM,
    GQA_KV_HEADS * GQA_HEAD_DIM,
    MLA_Q_RANK,
    MLA_KV_RANK,
    MLA_ROPE_DIM,
    N_BRANCH * D_MODEL,
)
D_IN = sum(IN_SIZES)

kernel_name = 'hybrid_conv_gqa_mla_encoder'


def rmsnorm(x, g):
    xf = x.astype(jnp.float32)
    y = xf * lax.rsqrt(jnp.mean(xf * xf, axis=-1, keepdims=True) + NORM_EPS)
    return (y * g.astype(jnp.float32)).astype(x.dtype)


def layernorm(x, g, b):
    xf = x.astype(jnp.float32)
    mu = jnp.mean(xf, axis=-1, keepdims=True)
    xc = xf - mu
    y = xc * lax.rsqrt(jnp.mean(xc * xc, axis=-1, keepdims=True) + NORM_EPS)
    return (y * g.astype(jnp.float32) + b.astype(jnp.float32)).astype(x.dtype)


def grid_positions(n_tok):
    rows = n_tok // GRID_W
    row = jnp.repeat(jnp.arange(rows, dtype=jnp.float32), GRID_W)
    col = jnp.tile(jnp.arange(GRID_W, dtype=jnp.float32), rows)
    meta = jnp.zeros((N_META,), jnp.float32)
    return jnp.concatenate([meta, row]), jnp.concatenate([meta, col])


def apply_rope(x, pos):
    dim = x.shape[-1]
    half = dim // 2
    inv = ROPE_THETA ** (-jnp.arange(half, dtype=jnp.float32) / half)
    ang = pos[:, None] * inv[None, :]
    cos = jnp.cos(ang)[None, :, None, :].astype(x.dtype)
    sin = jnp.sin(ang)[None, :, None, :].astype(x.dtype)
    x1, x2 = x[..., :half], x[..., half:]
    return jnp.concatenate([x1 * cos - x2 * sin, x2 * cos + x1 * sin], axis=-1)


def axial_rope(x, row_pos, col_pos):
    half = x.shape[-1] // 2
    return jnp.concatenate([apply_rope(x[..., :half], row_pos),
                            apply_rope(x[..., half:], col_pos)], axis=-1)


def _attend(qb, k, v, scale):
    s = jnp.einsum('bqhgd,bkhd->bhgqk', qb, k, preferred_element_type=jnp.float32) * scale
    p = jax.nn.softmax(s, axis=-1).astype(v.dtype)
    return jnp.einsum('bhgqk,bkhe->bqhge', p, v)


def bidir_block_attention(q, k, v, scale):
    b, l, hq, dk = q.shape
    hkv, dv = k.shape[2], v.shape[-1]
    g = hq // hkv
    q = q.reshape(b, l, hkv, g, dk)
    out_meta = _attend(q[:, :N_META], k, v, scale)
    n_tok = l - N_META
    nblk = n_tok // Q_BLOCK
    qr = q[:, N_META:].reshape(b, nblk, Q_BLOCK, hkv, g, dk).transpose(1, 0, 2, 3, 4, 5)
    out_r = lax.map(lambda qb: _attend(qb, k, v, scale), qr)
    out_r = out_r.transpose(1, 0, 2, 3, 4, 5).reshape(b, n_tok, hkv, g, dv)
    out = jnp.concatenate([out_meta, out_r], axis=1)
    return out.reshape(b, l, hq * dv)


def conv_branch(u2, dw, cb, ln_g, ln_b, w_pw):
    a, gte = jnp.split(u2, 2, axis=-1)
    z = a * jax.nn.sigmoid(gte)
    z = lax.conv_general_dilated(z, dw, window_strides=(1,),
                                 padding=[(CONV_K // 2, CONV_K // 2)],
                                 dimension_numbers=('NWC', 'WIO', 'NWC'),
                                 feature_group_count=C_CONV) + cb
    z = jax.nn.silu(layernorm(z, ln_g, ln_b))
    return z @ w_pw


def gqa_branch(q, k, v, qn_g, kn_g, w_o, row_pos, col_pos):
    b, l, _ = q.shape
    q = rmsnorm(q.reshape(b, l, GQA_HEADS, GQA_HEAD_DIM), qn_g)
    k = rmsnorm(k.reshape(b, l, GQA_KV_HEADS, GQA_HEAD_DIM), kn_g)
    v = v.reshape(b, l, GQA_KV_HEADS, GQA_HEAD_DIM)
    q = axial_rope(q, row_pos, col_pos)
    k = axial_rope(k, row_pos, col_pos)
    o = bidir_block_attention(q, k, v, 1.0 / math.sqrt(GQA_HEAD_DIM))
    return o @ w_o


def mla_branch(cq, ckv, kpe, qn_g, w_uq, kvn_g, w_ukv, w_o, row_pos, col_pos):
    b, l, _ = cq.shape
    q = (rmsnorm(cq, qn_g) @ w_uq).reshape(b, l, MLA_HEADS, MLA_QK_DIM)
    q_nope, q_pe = q[..., :MLA_NOPE_DIM], q[..., MLA_NOPE_DIM:]
    q_pe = axial_rope(q_pe, row_pos, col_pos)
    kv = (rmsnorm(ckv, kvn_g) @ w_ukv).reshape(b, l, MLA_HEADS, MLA_NOPE_DIM + MLA_V_DIM)
    k_nope, v = kv[..., :MLA_NOPE_DIM], kv[..., MLA_NOPE_DIM:]
    k_pe = axial_rope(kpe.reshape(b, l, 1, MLA_ROPE_DIM), row_pos, col_pos)
    k = jnp.concatenate([k_nope, jnp.broadcast_to(k_pe, (b, l, MLA_HEADS, MLA_ROPE_DIM))], axis=-1)
    q = jnp.concatenate([q_nope, q_pe], axis=-1)
    o = bidir_block_attention(q, k, v, 1.0 / math.sqrt(MLA_QK_DIM))
    return o @ w_o


def setup_inputs(seed: int = 0) -> dict:
    key = jax.random.key(seed)
    ks = jax.random.split(key, 24)

    def nrm(k, shape, scale):
        return jax.random.normal(k, shape, jnp.float32) * scale

    def gain(k, shape):
        return 1.0 + 0.02 * jax.random.normal(k, shape, jnp.float32)

    L = DEPTH
    return {
        'x': nrm(ks[0], (BATCH, SEQ, D_MODEL), 1.0),
        'meta_tokens': nrm(ks[1], (N_META, D_MODEL), 1.0),
        'mix_norm_g': gain(ks[2], (L, D_MODEL)),
        'w_in': nrm(ks[3], (L, D_MODEL, D_IN), D_MODEL ** -0.5),
        'conv_dw': nrm(ks[4], (L, CONV_K, 1, C_CONV), CONV_K ** -0.5),
        'conv_b': nrm(ks[5], (L, C_CONV), 0.02),
        'conv_ln_g': gain(ks[6], (L, C_CONV)),
        'conv_ln_b': nrm(ks[7], (L, C_CONV), 0.02),
        'w_conv_out': nrm(ks[8], (L, C_CONV, D_MODEL), C_CONV ** -0.5),
        'gqa_q_norm_g': gain(ks[9], (L, GQA_HEAD_DIM)),
        'gqa_k_norm_g': gain(ks[10], (L, GQA_HEAD_DIM)),
        'w_gqa_out': nrm(ks[11], (L, GQA_HEADS * GQA_HEAD_DIM, D_MODEL), (GQA_HEADS * GQA_HEAD_DIM) ** -0.5),
        'mla_q_norm_g': gain(ks[12], (L, MLA_Q_RANK)),
        'w_mla_uq': nrm(ks[13], (L, MLA_Q_RANK, MLA_HEADS * MLA_QK_DIM), MLA_Q_RANK ** -0.5),
        'mla_kv_norm_g': gain(ks[14], (L, MLA_KV_RANK)),
        'w_mla_ukv': nrm(ks[15], (L, MLA_KV_RANK, MLA_HEADS * (MLA_NOPE_DIM + MLA_V_DIM)), MLA_KV_RANK ** -0.5),
        'w_mla_out': nrm(ks[16], (L, MLA_HEADS * MLA_V_DIM, D_MODEL), (MLA_HEADS * MLA_V_DIM) ** -0.5),
        'gate_b': nrm(ks[17], (L, N_BRANCH * D_MODEL), 0.1),
        'w_out': nrm(ks[18], (L, D_MODEL, D_MODEL), 0.5 * D_MODEL ** -0.5),
        'ffn_norm_g': gain(ks[19], (L, D_MODEL)),
        'w_ffn_gate': nrm(ks[20], (L, D_MODEL, D_FF), D_MODEL ** -0.5),
        'w_ffn_up': nrm(ks[21], (L, D_MODEL, D_FF), D_MODEL ** -0.5),
        'w_ffn_down': nrm(ks[22], (L, D_FF, D_MODEL), 0.5 * D_FF ** -0.5),
        'final_norm_g': gain(ks[23], (D_MODEL,)),
    }


def reference(x, meta_tokens, mix_norm_g, w_in, conv_dw, conv_b, conv_ln_g, conv_ln_b, w_conv_out,
              gqa_q_norm_g, gqa_k_norm_g, w_gqa_out, mla_q_norm_g, w_mla_uq, mla_kv_norm_g,
              w_mla_ukv, w_mla_out, gate_b, w_out, ffn_norm_g, w_ffn_gate, w_ffn_up, w_ffn_down,
              final_norm_g):
    b, n_tok, d = x.shape
    meta = jnp.broadcast_to(meta_tokens.astype(x.dtype)[None], (b, N_META, d))
    h = jnp.concatenate([meta, x], axis=1)
    row_pos, col_pos = grid_positions(n_tok)

    split_points = []
    acc = 0
    for s in IN_SIZES[:-1]:
        acc += s
        split_points.append(acc)

    for i in range(DEPTH):
        u = rmsnorm(h, mix_norm_g[i])
        proj = u @ w_in[i]
        (u_conv, q_g, k_g, v_g, c_q, c_kv, k_pe, gate_logits) = jnp.split(proj, split_points, axis=-1)

        y_a = conv_branch(u_conv, conv_dw[i], conv_b[i], conv_ln_g[i], conv_ln_b[i], w_conv_out[i])
        y_b = gqa_branch(q_g, k_g, v_g, gqa_q_norm_g[i], gqa_k_norm_g[i], w_gqa_out[i], row_pos, col_pos)
        y_c = mla_branch(c_q, c_kv, k_pe, mla_q_norm_g[i], w_mla_uq[i], mla_kv_norm_g[i],
                         w_mla_ukv[i], w_mla_out[i], row_pos, col_pos)

        gates = jax.nn.sigmoid(gate_logits + gate_b[i]).reshape(b, -1, N_BRANCH, d)
        merged = gates[:, :, 0] * y_a + gates[:, :, 1] * y_b + gates[:, :, 2] * y_c
        h = h + merged @ w_out[i]

        v = rmsnorm(h, ffn_norm_g[i])
        h = h + (jax.nn.silu(v @ w_ffn_gate[i]) * (v @ w_ffn_up[i])) @ w_ffn_down[i]

    h = rmsnorm(h, final_norm_g)
    return h[:, N_META:]
```

```python
import functools
import math

import jax
import jax.numpy as jnp
from jax import lax
from jax.experimental import pallas as pl
from jax.experimental.pallas import tpu as pltpu

D_MODEL = 2048
N_META = 16
GRID_W = 64
ROPE_THETA = 10000.0
NORM_EPS = 1e-6
C_CONV = D_MODEL // 2
CONV_K = 31
GQA_HEADS = 8
GQA_KV_HEADS = 2
GQA_GROUP = GQA_HEADS // GQA_KV_HEADS
HEAD_DIM = 128
MLA_HEADS = 8
MLA_RANK = D_MODEL // 4
MLA_NOPE = 128
MLA_ROPE = 64
MLA_QK = MLA_NOPE + MLA_ROPE
D_FF = 5632

LANES = 128
MXU_DIM = 256
VMEM_LIMIT = 56 * 1024 * 1024

ROW_TILE = 768
MERGE_ROW_TILE = 256
FF_TILE = 512
Q_TILE = 688
CONV_ROWS = 48
CONV_PAD = 16

A_Q, A_CQ, A_CKV, A_K, A_V, A_KPE, A_END = 0, 1024, 1536, 2048, 2304, 2560, 2688
A_COLS = 2816
A_TILE = A_COLS // 2

F32 = jnp.float32
BF16 = jnp.bfloat16
NEG_BIG = -1e30


def _cparams(n_axes):
    return pltpu.CompilerParams(
        dimension_semantics=("arbitrary",) * n_axes, vmem_limit_bytes=VMEM_LIMIT)


def _rms(x, g):
    return x * lax.rsqrt(jnp.mean(x * x, axis=-1, keepdims=True) + NORM_EPS) * g


def _rope(x, cos, sin_lo, sin_hi, shift):
    up = pltpu.roll(x, LANES - shift, 1)
    dn = pltpu.roll(x, shift, 1)
    return x * cos + up * sin_lo + dn * sin_hi


def _dot_nt(a, b):
    return lax.dot_general(a, b, (((1,), (1,)), ((), ())), preferred_element_type=F32)


def _dot(a, b):
    return jnp.dot(a, b, preferred_element_type=F32)


def _rms_matmul_kernel(x_ref, g_ref, w_ref, *rest, epilogue):
    if epilogue == "sigmoid_bias":
        b_ref, o_ref, xn_ref = rest
    else:
        o_ref, xn_ref = rest

    @pl.when(pl.program_id(1) == 0)
    def _():
        xn_ref[...] = _rms(x_ref[...].astype(F32), g_ref[...]).astype(BF16)

    acc = _dot(xn_ref[...], w_ref[...])
    if epilogue == "glu":
        half = acc.shape[1] // 2
        acc = acc[:, :half] * jax.nn.sigmoid(acc[:, half:])
    elif epilogue == "sigmoid_bias":
        acc = jax.nn.sigmoid(acc + b_ref[...])
    o_ref[...] = acc.astype(o_ref.dtype)


def _rms_matmul(x, g, w, *, bn, x_col_block=0, k_dim=None, bias=None, epilogue="none"):
    t = x.shape[0]
    k_dim = k_dim or x.shape[1]
    n = w.shape[1]
    out_bn = bn // 2 if epilogue == "glu" else bn
    out_n = n // 2 if epilogue == "glu" else n
    in_specs = [
        pl.BlockSpec((ROW_TILE, k_dim), lambda i, j: (i, x_col_block)),
        pl.BlockSpec((1, k_dim), lambda i, j: (0, 0)),
        pl.BlockSpec((k_dim, bn), lambda i, j: (0, j)),
    ]
    args = [x, g.reshape(1, k_dim), w]
    if epilogue == "sigmoid_bias":
        in_specs.append(pl.BlockSpec((1, bn), lambda i, j: (0, j)))
        args.append(bias.reshape(1, n))
    return pl.pallas_call(
        functools.partial(_rms_matmul_kernel, epilogue=epilogue),
        grid=(t // ROW_TILE, n // bn),
        in_specs=in_specs,
        out_specs=pl.BlockSpec((ROW_TILE, out_bn), lambda i, j: (i, j)),
        out_shape=jax.ShapeDtypeStruct((t, out_n), BF16),
        scratch_shapes=[pltpu.VMEM((ROW_TILE, k_dim), BF16)],
        compiler_params=_cparams(2),
        name="rms_matmul_" + epilogue,
    )(*args)


def _conv_kernel(z_ref, dw_ref, cb_ref, o_ref, zp_ref, *, seq):
    tail = zp_ref.shape[0] - CONV_PAD - seq
    zp_ref[0:CONV_PAD, :] = jnp.zeros((CONV_PAD, LANES), F32)
    zp_ref[CONV_PAD + seq:, :] = jnp.zeros((tail, LANES), F32)
    zp_ref[CONV_PAD:CONV_PAD + seq, :] = z_ref[...].astype(F32)
    bias = cb_ref[...]

    def chunk(c, carry):
        base = pl.multiple_of(c * CONV_ROWS, 8)
        acc = jnp.zeros((CONV_ROWS, LANES), F32) + bias
        for j in range(CONV_K):
            off = CONV_PAD - CONV_K // 2 + j
            acc = acc + zp_ref[pl.ds(base + off, CONV_ROWS), :] * dw_ref[j:j + 1, :]
        o_ref[pl.ds(base, CONV_ROWS), :] = acc.astype(o_ref.dtype)
        return carry

    lax.fori_loop(0, seq // CONV_ROWS, chunk, 0)


def _depthwise_conv(z, dw, cb, batch, seq):
    c = z.shape[1]
    z3 = z.reshape(batch, seq, c)
    pad_rows = CONV_PAD + seq + CONV_PAD
    out = pl.pallas_call(
        functools.partial(_conv_kernel, seq=seq),
        grid=(batch, c // LANES),
        in_specs=[
            pl.BlockSpec((None, seq, LANES), lambda b, j: (b, 0, j)),
            pl.BlockSpec((CONV_K, LANES), lambda b, j: (0, j)),
            pl.BlockSpec((1, LANES), lambda b, j: (0, j)),
        ],
        out_specs=pl.BlockSpec((None, seq, LANES), lambda b, j: (b, 0, j)),
        out_shape=jax.ShapeDtypeStruct((batch, seq, c), BF16),
        scratch_shapes=[pltpu.VMEM((pad_rows, LANES), F32)],
        compiler_params=_cparams(2),
        name="depthwise_conv",
    )(z3, dw, cb.reshape(1, c))
    return out.reshape(batch * seq, c)


def _softmax_pv(q, k_s, v_s, n_real, n_meta):
    s_main = _dot_nt(q, k_s[0:n_real, :])
    lane = lax.broadcasted_iota(jnp.int32, (1, LANES), 1)
    tail_bias = jnp.where(lane < n_meta, 0.0, NEG_BIG).astype(F32)
    s_tail = _dot_nt(q, k_s[n_real:n_real + LANES, :]) + tail_bias
    m = jnp.maximum(jnp.max(s_main, axis=-1, keepdims=True),
                    jnp.max(s_tail, axis=-1, keepdims=True))
    p_main = jnp.exp(s_main - m)
    p_tail = jnp.exp(s_tail - m)
    denom = jnp.sum(p_main, axis=-1, keepdims=True) + jnp.sum(p_tail, axis=-1, keepdims=True)
    o = _dot(p_main.astype(BF16), v_s[0:n_real, :]) + _dot(p_tail.astype(BF16), v_s[n_real:n_real + LANES, :])
    return o / denom


def _fill_keys(dst_ref, col, make_rows, n_real, n_meta):
    dst_ref[0:n_real, col:col + LANES] = make_rows(n_meta, n_real)
    dst_ref[n_real:n_real + n_meta, col:col + LANES] = make_rows(0, n_meta)
    dst_ref[n_real + n_meta:n_real + LANES, col:col + LANES] = jnp.zeros((LANES - n_meta, LANES), BF16)


def _gqa_kernel(q_ref, k_ref, v_ref, qcos_ref, qlo_ref, qhi_ref, kcos_ref, klo_ref, khi_ref,
                qg_ref, kg_ref, o_ref, k_s, v_s, *, n_real, n_meta):
    head = pl.program_id(1)
    qt = pl.program_id(2)
    shift = HEAD_DIM // 4

    @pl.when((lax.rem(head, GQA_GROUP) == 0) & (qt == 0))
    def _():
        def k_rows(start, size):
            x = _rms(k_ref[start:start + size, :].astype(F32), kg_ref[...])
            x = _rope(x, kcos_ref[start:start + size, :], klo_ref[start:start + size, :],
                      khi_ref[start:start + size, :], shift)
            return x.astype(BF16)

        def v_rows(start, size):
            return v_ref[start:start + size, :]

        _fill_keys(k_s, 0, k_rows, n_real, n_meta)
        _fill_keys(v_s, 0, v_rows, n_real, n_meta)

    q = _rms(q_ref[...].astype(F32), qg_ref[...])
    q = _rope(q, qcos_ref[...], qlo_ref[...], qhi_ref[...], shift)
    q = (q * (1.0 / math.sqrt(HEAD_DIM))).astype(BF16)
    o_ref[...] = _softmax_pv(q, k_s, v_s, n_real, n_meta).astype(o_ref.dtype)


def _gqa_attention(pa, tabs, qn_g, kn_g, batch, seq):
    pa3 = pa.reshape(batch, seq, A_COLS)
    n_real = seq - N_META
    q_blk, k_blk, v_blk = A_Q // LANES, A_K // LANES, A_V // LANES
    tq = pl.BlockSpec((Q_TILE, LANES), lambda b, h, t: (t, 0))
    tk = pl.BlockSpec((seq, LANES), lambda b, h, t: (0, 0))
    gspec = pl.BlockSpec((1, LANES), lambda b, h, t: (0, 0))
    out = pl.pallas_call(
        functools.partial(_gqa_kernel, n_real=n_real, n_meta=N_META),
        grid=(batch, GQA_HEADS, seq // Q_TILE),
        in_specs=[
            pl.BlockSpec((None, Q_TILE, LANES), lambda b, h, t: (b, t, q_blk + h)),
            pl.BlockSpec((None, seq, LANES), lambda b, h, t: (b, 0, k_blk + h // GQA_GROUP)),
            pl.BlockSpec((None, seq, LANES), lambda b, h, t: (b, 0, v_blk + h // GQA_GROUP)),
            tq, tq, tq, tk, tk, tk, gspec, gspec,
        ],
        out_specs=pl.BlockSpec((None, Q_TILE, LANES), lambda b, h, t: (b, t, h)),
        out_shape=jax.ShapeDtypeStruct((batch, seq, GQA_HEADS * HEAD_DIM), BF16),
        scratch_shapes=[pltpu.VMEM((n_real + LANES, HEAD_DIM), BF16),
                        pltpu.VMEM((n_real + LANES, HEAD_DIM), BF16)],
        compiler_params=_cparams(3),
        name="gqa_attention",
    )(pa3, pa3, pa3, tabs[0], tabs[1], tabs[2], tabs[0], tabs[1], tabs[2],
      qn_g.reshape(1, HEAD_DIM), kn_g.reshape(1, HEAD_DIM))
    return out.reshape(batch * seq, GQA_HEADS * HEAD_DIM)


def _mla_kernel(q_ref, kv_ref, kpe_ref, qcos_ref, qlo_ref, qhi_ref, kcos_ref, klo_ref, khi_ref,
                o_ref, k_s, v_s, *, n_real, n_meta):
    qt = pl.program_id(2)
    shift = MLA_ROPE // 4

    @pl.when(qt == 0)
    def _():
        def knope_rows(start, size):
            return kv_ref[start:start + size, 0:LANES]

        def kpe_rows(start, size):
            x = _rope(kpe_ref[start:start + size, :].astype(F32), kcos_ref[start:start + size, :],
                      klo_ref[start:start + size, :], khi_ref[start:start + size, :], shift)
            return x.astype(BF16)

        def v_rows(start, size):
            return kv_ref[start:start + size, LANES:2 * LANES]

        _fill_keys(k_s, 0, knope_rows, n_real, n_meta)
        _fill_keys(k_s, LANES, kpe_rows, n_real, n_meta)
        _fill_keys(v_s, 0, v_rows, n_real, n_meta)

    scale = 1.0 / math.sqrt(MLA_QK)
    q_nope = q_ref[:, 0:LANES].astype(F32) * scale
    q_pe = _rope(q_ref[:, LANES:2 * LANES].astype(F32), qcos_ref[...], qlo_ref[...], qhi_ref[...], shift) * scale
    q = jnp.concatenate([q_nope, q_pe], axis=-1).astype(BF16)
    o_ref[...] = _softmax_pv(q, k_s, v_s, n_real, n_meta).astype(o_ref.dtype)


def _mla_attention(q_mla, kv_mla, pa, tabs, batch, seq):
    n_real = seq - N_META
    width = 2 * LANES
    q3 = q_mla.reshape(batch, seq, MLA_HEADS * width)
    kv3 = kv_mla.reshape(batch, seq, MLA_HEADS * width)
    pa3 = pa.reshape(batch, seq, A_COLS)
    kpe_blk = A_KPE // LANES
    tq = pl.BlockSpec((Q_TILE, LANES), lambda b, h, t: (t, 0))
    tk = pl.BlockSpec((seq, LANES), lambda b, h, t: (0, 0))
    out = pl.pallas_call(
        functools.partial(_mla_kernel, n_real=n_real, n_meta=N_META),
        grid=(batch, MLA_HEADS, seq // Q_TILE),
        in_specs=[
            pl.BlockSpec((None, Q_TILE, width), lambda b, h, t: (b, t, h)),
            pl.BlockSpec((None, seq, width), lambda b, h, t: (b, 0, h)),
            pl.BlockSpec((None, seq, LANES), lambda b, h, t: (b, 0, kpe_blk)),
            tq, tq, tq, tk, tk, tk,
        ],
        out_specs=pl.BlockSpec((None, Q_TILE, LANES), lambda b, h, t: (b, t, h)),
        out_shape=jax.ShapeDtypeStruct((batch, seq, MLA_HEADS * LANES), BF16),
        scratch_shapes=[pltpu.VMEM((n_real + LANES, width), BF16),
                        pltpu.VMEM((n_real + LANES, LANES), BF16)],
        compiler_params=_cparams(3),
        name="mla_attention",
    )(q3, kv3, pa3, tabs[0], tabs[1], tabs[2], tabs[0], tabs[1], tabs[2])
    return out.reshape(batch * seq, MLA_HEADS * LANES)


def _merge_kernel(c_ref, ob_ref, oc_ref, g0_ref, g1_ref, g2_ref, h_ref, lng_ref, lnb_ref,
                  wa_ref, wb_ref, wc_ref, wo_ref, o_ref):
    c = c_ref[...].astype(F32)
    mu = jnp.mean(c, axis=-1, keepdims=True)
    xc = c - mu
    y = xc * lax.rsqrt(jnp.mean(xc * xc, axis=-1, keepdims=True) + NORM_EPS)
    y = y * lng_ref[...] + lnb_ref[...]
    act = (y * jax.nn.sigmoid(y)).astype(BF16)
    merged = g0_ref[...].astype(F32) * _dot(act, wa_ref[...])
    merged = merged + g1_ref[...].astype(F32) * _dot(ob_ref[...], wb_ref[...])
    merged = merged + g2_ref[...].astype(F32) * _dot(oc_ref[...], wc_ref[...])
    o_ref[...] = h_ref[...] + _dot(merged.astype(BF16), wo_ref[...])


def _merge(c, o_b, o_c, gates, h, ln_g, ln_b, w_a, w_b, w_c, w_o):
    t, d = h.shape
    bm = MERGE_ROW_TILE
    row = lambda width, col: pl.BlockSpec((bm, width), lambda i: (i, col))
    const = lambda shape: pl.BlockSpec(shape, lambda i: (0, 0), pipeline_mode=pl.Buffered(1))
    return pl.pallas_call(
        _merge_kernel,
        grid=(t // bm,),
        in_specs=[row(C_CONV, 0), row(C_CONV, 0), row(C_CONV, 0),
                  row(d, 0), row(d, 1), row(d, 2), row(d, 0),
                  const((1, C_CONV)), const((1, C_CONV)),
                  const((C_CONV, d)), const((C_CONV, d)), const((C_CONV, d)), const((d, d))],
        out_specs=row(d, 0),
        out_shape=jax.ShapeDtypeStruct((t, d), F32),
        compiler_params=_cparams(1),
        name="branch_merge",
    )(c, o_b, o_c, gates, gates, gates, h, ln_g.reshape(1, C_CONV), ln_b.reshape(1, C_CONV),
      w_a, w_b, w_c, w_o)


def _ffn_kernel(h_ref, g_ref, wg_ref, wu_ref, wd_ref, o_ref, v_ref):
    @pl.when(pl.program_id(1) == 0)
    def _():
        hv = h_ref[...]
        v_ref[...] = _rms(hv, g_ref[...]).astype(BF16)
        o_ref[...] = hv

    v = v_ref[...]
    gate = _dot(v, wg_ref[...])
    up = _dot(v, wu_ref[...])
    a = (gate * jax.nn.sigmoid(gate) * up).astype(BF16)
    o_ref[...] += _dot(a, wd_ref[...])


def _ffn(h, g, w_gate, w_up, w_down):
    t, d = h.shape
    return pl.pallas_call(
        _ffn_kernel,
        grid=(t // ROW_TILE, D_FF // FF_TILE),
        in_specs=[
            pl.BlockSpec((ROW_TILE, d), lambda i, f: (i, 0)),
            pl.BlockSpec((1, d), lambda i, f: (0, 0)),
            pl.BlockSpec((d, FF_TILE), lambda i, f: (0, f)),
            pl.BlockSpec((d, FF_TILE), lambda i, f: (0, f)),
            pl.BlockSpec((FF_TILE, d), lambda i, f: (f, 0)),
        ],
        out_specs=pl.BlockSpec((ROW_TILE, d), lambda i, f: (i, 0)),
        out_shape=jax.ShapeDtypeStruct((t, d), F32),
        scratch_shapes=[pltpu.VMEM((ROW_TILE, d), BF16)],
        compiler_params=_cparams(2),
        name="swiglu_ffn",
    )(h, g.reshape(1, d), w_gate, w_up, w_down)


def _final_norm_kernel(h_ref, g_ref, o_ref):
    o_ref[...] = _rms(h_ref[...], g_ref[...])


def _final_norm(h, g):
    t, d = h.shape
    return pl.pallas_call(
        _final_norm_kernel,
        grid=(t // ROW_TILE,),
        in_specs=[pl.BlockSpec((ROW_TILE, d), lambda i: (i, 0)),
                  pl.BlockSpec((1, d), lambda i: (0, 0))],
        out_specs=pl.BlockSpec((ROW_TILE, d), lambda i: (i, 0)),
        out_shape=jax.ShapeDtypeStruct((t, d), F32),
        compiler_params=_cparams(1),
        name="final_rmsnorm",
    )(h, g.reshape(1, d))


def _rope_tables(n_tok, rot_dim):
    rows = n_tok // GRID_W
    row = jnp.concatenate([jnp.zeros((N_META,), F32), jnp.repeat(jnp.arange(rows, dtype=F32), GRID_W)])
    col = jnp.concatenate([jnp.zeros((N_META,), F32), jnp.tile(jnp.arange(GRID_W, dtype=F32), rows)])
    sub = rot_dim // 4
    inv = ROPE_THETA ** (-jnp.arange(sub, dtype=F32) / sub)
    ang_r = row[:, None] * inv[None, :]
    ang_c = col[:, None] * inv[None, :]
    cr, sr, cc, sc = jnp.cos(ang_r), jnp.sin(ang_r), jnp.cos(ang_c), jnp.sin(ang_c)
    zero = jnp.zeros_like(sr)
    l = row.shape[0]
    pad1 = jnp.ones((l, LANES - rot_dim), F32)
    pad0 = jnp.zeros((l, LANES - rot_dim), F32)
    cos = jnp.concatenate([cr, cr, cc, cc, pad1], axis=1)
    sin_lo = jnp.concatenate([-sr, zero, -sc, zero, pad0], axis=1)
    sin_hi = jnp.concatenate([zero, sr, zero, sc, pad0], axis=1)
    return cos, sin_lo, sin_hi


def _pack_layer_weights(w_in, w_uq):
    o = 0
    a, o = w_in[:, o:o + C_CONV], o + C_CONV
    gte, o = w_in[:, o:o + C_CONV], o + C_CONV
    q, o = w_in[:, o:o + 1024], o + 1024
    k, o = w_in[:, o:o + 256], o + 256
    v, o = w_in[:, o:o + 256], o + 256
    cq, o = w_in[:, o:o + MLA_RANK], o + MLA_RANK
    ckv, o = w_in[:, o:o + MLA_RANK], o + MLA_RANK
    kpe, o = w_in[:, o:o + MLA_ROPE], o + MLA_ROPE
    gates = w_in[:, o:]
    half = C_CONV // 2
    w_glu = jnp.concatenate([a[:, :half], gte[:, :half], a[:, half:], gte[:, half:]], axis=1)
    zeros = jnp.zeros((D_MODEL, A_COLS - A_END + LANES - MLA_ROPE), w_in.dtype)
    w_attn = jnp.concatenate([q, cq, ckv, k, v, kpe, zeros], axis=1)
    wq = w_uq.reshape(MLA_RANK, MLA_HEADS, MLA_QK)
    wq = jnp.concatenate([wq, jnp.zeros((MLA_RANK, MLA_HEADS, 2 * LANES - MLA_QK), w_uq.dtype)], axis=2)
    wq = wq.reshape(MLA_RANK, MLA_HEADS * 2 * LANES)
    return w_glu.astype(BF16), gates.astype(BF16), w_attn.astype(BF16), wq.astype(BF16)


def kernel(x, meta_tokens, mix_norm_g, w_in, conv_dw, conv_b, conv_ln_g, conv_ln_b, w_conv_out, gqa_q_norm_g, gqa_k_norm_g, w_gqa_out, mla_q_norm_g, w_mla_uq, mla_kv_norm_g, w_mla_ukv, w_mla_out, gate_b, w_out, ffn_norm_g, w_ffn_gate, w_ffn_up, w_ffn_down, final_norm_g):
    batch, n_tok, d = x.shape
    seq = N_META + n_tok
    depth = w_in.shape[0]
    assert d == D_MODEL and seq % Q_TILE == 0 and seq % CONV_ROWS == 0
    assert (batch * seq) % ROW_TILE == 0 and (batch * seq) % MERGE_ROW_TILE == 0

    meta = jnp.broadcast_to(meta_tokens.astype(x.dtype)[None], (batch, N_META, d))
    h = jnp.concatenate([meta, x], axis=1).reshape(batch * seq, d)

    tabs_gqa = _rope_tables(n_tok, HEAD_DIM)
    tabs_mla = _rope_tables(n_tok, MLA_ROPE)

    for i in range(depth):
        w_glu, w_gates, w_attn, w_uq = _pack_layer_weights(w_in[i], w_mla_uq[i])
        g_mix = mix_norm_g[i]

        z = _rms_matmul(h, g_mix, w_glu, bn=1024, epilogue="glu")
        gates = _rms_matmul(h, g_mix, w_gates, bn=1024, bias=gate_b[i], epilogue="sigmoid_bias")
        pa = _rms_matmul(h, g_mix, w_attn, bn=A_TILE)

        c = _depthwise_conv(z, conv_dw[i].reshape(CONV_K, C_CONV), conv_b[i], batch, seq)

        o_b = _gqa_attention(pa, tabs_gqa, gqa_q_norm_g[i], gqa_k_norm_g[i], batch, seq)

        q_mla = _rms_matmul(pa, mla_q_norm_g[i], w_uq, bn=w_uq.shape[1],
                            x_col_block=A_CQ // MLA_RANK, k_dim=MLA_RANK)
        kv_mla = _rms_matmul(pa, mla_kv_norm_g[i], w_mla_ukv[i].astype(BF16), bn=MLA_HEADS * 2 * LANES,
                             x_col_block=A_CKV // MLA_RANK, k_dim=MLA_RANK)
        o_c = _mla_attention(q_mla, kv_mla, pa, tabs_mla, batch, seq)

        h = _merge(c, o_b, o_c, gates, h, conv_ln_g[i], conv_ln_b[i],
                   w_conv_out[i].astype(BF16), w_gqa_out[i].astype(BF16), w_mla_out[i].astype(BF16),
                   w_out[i].astype(BF16))

        h = _ffn(h, ffn_norm_g[i], w_ffn_gate[i].astype(BF16), w_ffn_up[i].astype(BF16),
                 w_ffn_down[i].astype(BF16))

    out = _final_norm(h, final_norm_g).reshape(batch, seq, d)
    return out[:, N_META:]
```

```python
import functools
import math

import jax
import jax.numpy as jnp
from jax import lax
from jax.experimental import pallas as pl
from jax.experimental.pallas import tpu as pltpu

D_MODEL = 2048
N_META = 16
GRID_W = 64
ROPE_THETA = 10000.0
NORM_EPS = 1e-6
C_CONV = D_MODEL // 2
CONV_K = 31
GQA_HEADS = 8
GQA_KV_HEADS = 2
GQA_GROUP = GQA_HEADS // GQA_KV_HEADS
HEAD_DIM = 128
MLA_HEADS = 8
MLA_RANK = D_MODEL // 4
MLA_NOPE = 128
MLA_ROPE = 64
MLA_QK = MLA_NOPE + MLA_ROPE
D_FF = 5632

LANES = 128
MXU_DIM = 256
VMEM_LIMIT = 56 * 1024 * 1024

ROW_TILE = 768
MERGE_ROW_TILE = 256
FF_TILE = 512
FINAL_ROWS = 512
Q_TILE = 2064
GQA_SUB_TILES = (176,) * 11 + (128,)
MLA_SUB_TILES = (416,) * 4 + (400,)
ATTN_RING = 4
ATTN_LOOKAHEAD = 2
LOG2E = 1.4426950408889634
CONV_ROWS = 48
CONV_PAD = 16

A_Q, A_CQ, A_CKV, A_K, A_V, A_KPE, A_END = 0, 1024, 1536, 2048, 2304, 2560, 2688
IN_TILE = 1024
A_COLS = 3 * IN_TILE
GLU_TILES = 2 * C_CONV // IN_TILE
GATE_TILES = 3 * D_MODEL // IN_TILE
ATTN_TILES = A_COLS // IN_TILE

F32 = jnp.float32
BF16 = jnp.bfloat16
NEG_BIG = -1e30


def _cparams(n_axes):
    return pltpu.CompilerParams(
        dimension_semantics=("arbitrary",) * n_axes, vmem_limit_bytes=VMEM_LIMIT)


def _rms(x, g):
    return x * lax.rsqrt(jnp.mean(x * x, axis=-1, keepdims=True) + NORM_EPS) * g


def _rope(x, cos, sin_lo, sin_hi, shift):
    up = pltpu.roll(x, LANES - shift, 1)
    dn = pltpu.roll(x, shift, 1)
    return x * cos + up * sin_lo + dn * sin_hi


def _dot_nt(a, b):
    return lax.dot_general(a, b, (((1,), (1,)), ((), ())), preferred_element_type=F32)


def _dot(a, b):
    return jnp.dot(a, b, preferred_element_type=F32)


def _rms_matmul_kernel(x_ref, g_ref, w_ref, o_ref):
    xn = _rms(x_ref[...].astype(F32), g_ref[...]).astype(BF16)
    o_ref[...] = _dot(xn, w_ref[...]).astype(o_ref.dtype)


def _rms_matmul(x, g, w, *, x_col_block, k_dim):
    t = x.shape[0]
    n = w.shape[1]
    return pl.pallas_call(
        _rms_matmul_kernel,
        grid=(t // ROW_TILE,),
        in_specs=[
            pl.BlockSpec((ROW_TILE, k_dim), lambda i: (i, x_col_block)),
            pl.BlockSpec((1, k_dim), lambda i: (0, 0)),
            pl.BlockSpec((k_dim, n), lambda i: (0, 0)),
        ],
        out_specs=pl.BlockSpec((ROW_TILE, n), lambda i: (i, 0)),
        out_shape=jax.ShapeDtypeStruct((t, n), BF16),
        compiler_params=_cparams(1),
        name="rms_matmul",
    )(x, g.reshape(1, k_dim), w)


def _in_proj_kernel(x_ref, g_ref, w_ref, b_ref, z_ref, gates_ref, pa_ref, xn_ref):
    j = pl.program_id(1)

    @pl.when(j == 0)
    def _():
        xn_ref[...] = _rms(x_ref[...], g_ref[...]).astype(BF16)

    @pl.when(j < GLU_TILES)
    def _():
        acc = _dot(xn_ref[...], w_ref[...])
        half = IN_TILE // 2
        z_ref[...] = (acc[:, :half] * jax.nn.sigmoid(acc[:, half:])).astype(z_ref.dtype)

    @pl.when((j >= GLU_TILES) & (j < GLU_TILES + GATE_TILES))
    def _():
        acc = _dot(xn_ref[...], w_ref[...])
        gates_ref[...] = jax.nn.sigmoid(acc + b_ref[...]).astype(gates_ref.dtype)

    @pl.when(j >= GLU_TILES + GATE_TILES)
    def _():
        pa_ref[...] = _dot(xn_ref[...], w_ref[...]).astype(pa_ref.dtype)


def _in_proj(h, g, w_all, bias_all):
    t, d = h.shape
    n_tiles = GLU_TILES + GATE_TILES + ATTN_TILES
    clamp = lambda j, lo, n: jnp.clip(j - lo, 0, n - 1)
    return pl.pallas_call(
        _in_proj_kernel,
        grid=(t // ROW_TILE, n_tiles),
        in_specs=[
            pl.BlockSpec((ROW_TILE, d), lambda i, j: (i, 0)),
            pl.BlockSpec((1, d), lambda i, j: (0, 0)),
            pl.BlockSpec((d, IN_TILE), lambda i, j: (0, j)),
            pl.BlockSpec((1, IN_TILE), lambda i, j: (0, j)),
        ],
        out_specs=[
            pl.BlockSpec((ROW_TILE, IN_TILE // 2), lambda i, j: (i, clamp(j, 0, GLU_TILES))),
            pl.BlockSpec((ROW_TILE, IN_TILE), lambda i, j: (i, clamp(j, GLU_TILES, GATE_TILES))),
            pl.BlockSpec((ROW_TILE, IN_TILE), lambda i, j: (i, clamp(j, GLU_TILES + GATE_TILES, ATTN_TILES))),
        ],
        out_shape=[jax.ShapeDtypeStruct((t, C_CONV), BF16),
                   jax.ShapeDtypeStruct((t, 3 * D_MODEL), BF16),
                   jax.ShapeDtypeStruct((t, A_COLS), BF16)],
        scratch_shapes=[pltpu.VMEM((ROW_TILE, d), BF16)],
        compiler_params=_cparams(2),
        name="in_proj",
    )(h, g.reshape(1, d), w_all, bias_all)


def _conv_kernel(z_ref, dw_ref, cb_ref, o_ref, zp_ref, *, seq):
    tail = zp_ref.shape[0] - CONV_PAD - seq
    zp_ref[0:CONV_PAD, :] = jnp.zeros((CONV_PAD, LANES), F32)
    zp_ref[CONV_PAD + seq:, :] = jnp.zeros((tail, LANES), F32)
    zp_ref[CONV_PAD:CONV_PAD + seq, :] = z_ref[...].astype(F32)
    bias = cb_ref[...]

    def chunk(c, carry):
        base = pl.multiple_of(c * CONV_ROWS, 8)
        acc = jnp.zeros((CONV_ROWS, LANES), F32) + bias
        for j in range(CONV_K):
            off = CONV_PAD - CONV_K // 2 + j
            acc = acc + zp_ref[pl.ds(base + off, CONV_ROWS), :] * dw_ref[j:j + 1, :]
        o_ref[pl.ds(base, CONV_ROWS), :] = acc.astype(o_ref.dtype)
        return carry

    lax.fori_loop(0, seq // CONV_ROWS, chunk, 0)


def _depthwise_conv(z, dw, cb, batch, seq):
    c = z.shape[1]
    z3 = z.reshape(batch, seq, c)
    pad_rows = CONV_PAD + seq + CONV_PAD
    out = pl.pallas_call(
        functools.partial(_conv_kernel, seq=seq),
        grid=(batch, c // LANES),
        in_specs=[
            pl.BlockSpec((None, seq, LANES), lambda b, j: (b, 0, j)),
            pl.BlockSpec((CONV_K, LANES), lambda b, j: (0, j)),
            pl.BlockSpec((1, LANES), lambda b, j: (0, j)),
        ],
        out_specs=pl.BlockSpec((None, seq, LANES), lambda b, j: (b, 0, j)),
        out_shape=jax.ShapeDtypeStruct((batch, seq, c), BF16),
        scratch_shapes=[pltpu.VMEM((pad_rows, LANES), F32)],
        compiler_params=_cparams(2),
        name="depthwise_conv",
    )(z3, dw, cb.reshape(1, c))
    return out.reshape(batch * seq, c)


def _softmax_pv(q, k_s, v_s, s_ref, p_ref, o_ref, n_real, n_meta, sub_tiles):
    lane = lax.broadcasted_iota(jnp.int32, (1, LANES), 1)
    tail_bias = jnp.where(lane < n_meta, 0.0, NEG_BIG).astype(F32)
    slot_rows = max(sub_tiles)
    starts = [sum(sub_tiles[:i]) for i in range(len(sub_tiles))]

    def scores(idx):
        r0, rows, b0 = starts[idx], sub_tiles[idx], (idx % ATTN_RING) * slot_rows
        qs = q[r0:r0 + rows, :]
        s_ref[b0:b0 + rows, 0:n_real] = _dot_nt(qs, k_s[0:n_real, :])
        s_ref[b0:b0 + rows, n_real:n_real + LANES] = _dot_nt(qs, k_s[n_real:n_real + LANES, :]) + tail_bias

    def finish(idx):
        r0, rows, b0 = starts[idx], sub_tiles[idx], (idx % ATTN_RING) * slot_rows
        s = s_ref[b0:b0 + rows, :]
        m = jnp.max(s, axis=-1, keepdims=True)
        p_ref[b0:b0 + rows, :] = jnp.exp2(s - m).astype(BF16)
        o_ext = _dot(p_ref[b0:b0 + rows, :], v_s[...])
        o_ref[r0:r0 + rows, :] = (o_ext[:, 0:LANES] / o_ext[:, LANES:2 * LANES]).astype(o_ref.dtype)

    n_sub = len(sub_tiles)
    for idx in range(min(ATTN_LOOKAHEAD, n_sub)):
        scores(idx)
    for idx in range(n_sub):
        if idx + ATTN_LOOKAHEAD < n_sub:
            scores(idx + ATTN_LOOKAHEAD)
        finish(idx)


def _fill_keys(dst_ref, col, make_rows, n_real, n_meta):
    dst_ref[0:n_real, col:col + LANES] = make_rows(n_meta, n_real)
    dst_ref[n_real:n_real + n_meta, col:col + LANES] = make_rows(0, n_meta)
    dst_ref[n_real + n_meta:n_real + LANES, col:col + LANES] = jnp.zeros((LANES - n_meta, LANES), BF16)


def _fill_ones(v_s):
    v_s[:, LANES:2 * LANES] = jnp.ones((v_s.shape[0], LANES), BF16)


def _gqa_kernel(q_ref, k_ref, v_ref, cos_ref, lo_ref, hi_ref, qg_ref, kg_ref, o_ref, k_s, v_s, s_ref, p_ref, *, n_real, n_meta):
    head = pl.program_id(1)
    shift = HEAD_DIM // 4

    @pl.when(lax.rem(head, GQA_GROUP) == 0)
    def _():
        def k_rows(start, size):
            x = _rms(k_ref[start:start + size, :].astype(F32), kg_ref[...])
            x = _rope(x, cos_ref[start:start + size, :], lo_ref[start:start + size, :],
                      hi_ref[start:start + size, :], shift)
            return x.astype(BF16)

        def v_rows(start, size):
            return v_ref[start:start + size, :]

        _fill_keys(k_s, 0, k_rows, n_real, n_meta)
        _fill_keys(v_s, 0, v_rows, n_real, n_meta)
        _fill_ones(v_s)

    q = _rms(q_ref[...].astype(F32), qg_ref[...])
    q = _rope(q, cos_ref[...], lo_ref[...], hi_ref[...], shift)
    q = (q * (LOG2E / math.sqrt(HEAD_DIM))).astype(BF16)
    _softmax_pv(q, k_s, v_s, s_ref, p_ref, o_ref, n_real, n_meta, GQA_SUB_TILES)


def _gqa_attention(pa, tabs, qn_g, kn_g, batch, seq):
    pa3 = pa.reshape(batch, seq, A_COLS)
    n_real = seq - N_META
    q_blk, k_blk, v_blk = A_Q // LANES, A_K // LANES, A_V // LANES
    tab = pl.BlockSpec((seq, LANES), lambda b, h: (0, 0))
    gspec = pl.BlockSpec((1, LANES), lambda b, h: (0, 0))
    out = pl.pallas_call(
        functools.partial(_gqa_kernel, n_real=n_real, n_meta=N_META),
        grid=(batch, GQA_HEADS),
        in_specs=[
            pl.BlockSpec((None, seq, LANES), lambda b, h: (b, 0, q_blk + h)),
            pl.BlockSpec((None, seq, LANES), lambda b, h: (b, 0, k_blk + h // GQA_GROUP)),
            pl.BlockSpec((None, seq, LANES), lambda b, h: (b, 0, v_blk + h // GQA_GROUP)),
            tab, tab, tab, gspec, gspec,
        ],
        out_specs=pl.BlockSpec((None, seq, LANES), lambda b, h: (b, 0, h)),
        out_shape=jax.ShapeDtypeStruct((batch, seq, GQA_HEADS * HEAD_DIM), BF16),
        scratch_shapes=[pltpu.VMEM((n_real + LANES, HEAD_DIM), BF16),
                        pltpu.VMEM((n_real + LANES, 2 * LANES), BF16),
                        pltpu.VMEM((ATTN_RING * max(GQA_SUB_TILES), n_real + LANES), F32),
                        pltpu.VMEM((ATTN_RING * max(GQA_SUB_TILES), n_real + LANES), BF16)],
        compiler_params=_cparams(2),
        name="gqa_attention",
    )(pa3, pa3, pa3, tabs[0], tabs[1], tabs[2],
      qn_g.reshape(1, HEAD_DIM), kn_g.reshape(1, HEAD_DIM))
    return out.reshape(batch * seq, GQA_HEADS * HEAD_DIM)


def _mla_kernel(q_ref, kv_ref, kpe_ref, cos_ref, lo_ref, hi_ref, o_ref, k_s, v_s, s_ref, p_ref, *, n_real, n_meta):
    shift = MLA_ROPE // 4

    def knope_rows(start, size):
        return kv_ref[start:start + size, 0:LANES]

    def kpe_rows(start, size):
        x = _rope(kpe_ref[start:start + size, :].astype(F32), cos_ref[start:start + size, :],
                  lo_ref[start:start + size, :], hi_ref[start:start + size, :], shift)
        return x.astype(BF16)

    def v_rows(start, size):
        return kv_ref[start:start + size, LANES:2 * LANES]

    _fill_keys(k_s, 0, knope_rows, n_real, n_meta)
    _fill_keys(k_s, LANES, kpe_rows, n_real, n_meta)
    _fill_keys(v_s, 0, v_rows, n_real, n_meta)
    _fill_ones(v_s)

    scale = LOG2E / math.sqrt(MLA_QK)
    q_nope = q_ref[:, 0:LANES].astype(F32) * scale
    q_pe = _rope(q_ref[:, LANES:2 * LANES].astype(F32), cos_ref[...], lo_ref[...], hi_ref[...], shift) * scale
    q = jnp.concatenate([q_nope, q_pe], axis=-1).astype(BF16)
    _softmax_pv(q, k_s, v_s, s_ref, p_ref, o_ref, n_real, n_meta, MLA_SUB_TILES)


def _mla_attention(q_mla, kv_mla, pa, tabs, batch, seq):
    n_real = seq - N_META
    width = 2 * LANES
    q3 = q_mla.reshape(batch, seq, MLA_HEADS * width)
    kv3 = kv_mla.reshape(batch, seq, MLA_HEADS * width)
    pa3 = pa.reshape(batch, seq, A_COLS)
    kpe_blk = A_KPE // LANES
    tab = pl.BlockSpec((seq, LANES), lambda b, h: (0, 0))
    out = pl.pallas_call(
        functools.partial(_mla_kernel, n_real=n_real, n_meta=N_META),
        grid=(batch, MLA_HEADS),
        in_specs=[
            pl.BlockSpec((None, seq, width), lambda b, h: (b, 0, h)),
            pl.BlockSpec((None, seq, width), lambda b, h: (b, 0, h)),
            pl.BlockSpec((None, seq, LANES), lambda b, h: (b, 0, kpe_blk)),
            tab, tab, tab,
        ],
        out_specs=pl.BlockSpec((None, seq, LANES), lambda b, h: (b, 0, h)),
        out_shape=jax.ShapeDtypeStruct((batch, seq, MLA_HEADS * LANES), BF16),
        scratch_shapes=[pltpu.VMEM((n_real + LANES, width), BF16),
                        pltpu.VMEM((n_real + LANES, 2 * LANES), BF16),
                        pltpu.VMEM((ATTN_RING * max(MLA_SUB_TILES), n_real + LANES), F32),
                        pltpu.VMEM((ATTN_RING * max(MLA_SUB_TILES), n_real + LANES), BF16)],
        compiler_params=_cparams(2),
        name="mla_attention",
    )(q3, kv3, pa3, tabs[0], tabs[1], tabs[2])
    return out.reshape(batch * seq, MLA_HEADS * LANES)


def _merge_kernel(c_ref, ob_ref, oc_ref, g0_ref, g1_ref, g2_ref, h_ref, lng_ref, lnb_ref,
                  wa_ref, wb_ref, wc_ref, wo_ref, o_ref):
    c = c_ref[...].astype(F32)
    mu = jnp.mean(c, axis=-1, keepdims=True)
    xc = c - mu
    y = xc * lax.rsqrt(jnp.mean(xc * xc, axis=-1, keepdims=True) + NORM_EPS)
    y = y * lng_ref[...] + lnb_ref[...]
    act = (y * jax.nn.sigmoid(y)).astype(BF16)
    merged = g0_ref[...].astype(F32) * _dot(act, wa_ref[...])
    merged = merged + g1_ref[...].astype(F32) * _dot(ob_ref[...], wb_ref[...])
    merged = merged + g2_ref[...].astype(F32) * _dot(oc_ref[...], wc_ref[...])
    o_ref[...] = h_ref[...] + _dot(merged.astype(BF16), wo_ref[...])


def _merge(c, o_b, o_c, gates, h, ln_g, ln_b, w_a, w_b, w_c, w_o):
    t, d = h.shape
    bm = MERGE_ROW_TILE
    row = lambda width, col: pl.BlockSpec((bm, width), lambda i: (i, col))
    const = lambda shape: pl.BlockSpec(shape, lambda i: (0, 0), pipeline_mode=pl.Buffered(1))
    return pl.pallas_call(
        _merge_kernel,
        grid=(t // bm,),
        in_specs=[row(C_CONV, 0), row(C_CONV, 0), row(C_CONV, 0),
                  row(d, 0), row(d, 1), row(d, 2), row(d, 0),
                  const((1, C_CONV)), const((1, C_CONV)),
                  const((C_CONV, d)), const((C_CONV, d)), const((C_CONV, d)), const((d, d))],
        out_specs=row(d, 0),
        out_shape=jax.ShapeDtypeStruct((t, d), F32),
        compiler_params=_cparams(1),
        name="branch_merge",
    )(c, o_b, o_c, gates, gates, gates, h, ln_g.reshape(1, C_CONV), ln_b.reshape(1, C_CONV),
      w_a, w_b, w_c, w_o)


def _ffn_kernel(h_ref, g_ref, wg_ref, wu_ref, wd_ref, o_ref, v_ref):
    @pl.when(pl.program_id(1) == 0)
    def _():
        hv = h_ref[...]
        v_ref[...] = _rms(hv, g_ref[...]).astype(BF16)
        o_ref[...] = hv

    v = v_ref[...]
    gate = _dot(v, wg_ref[...])
    up = _dot(v, wu_ref[...])
    a = (gate * jax.nn.sigmoid(gate) * up).astype(BF16)
    o_ref[...] += _dot(a, wd_ref[...])


def _ffn(h, g, w_gate, w_up, w_down):
    t, d = h.shape
    return pl.pallas_call(
        _ffn_kernel,
        grid=(t // ROW_TILE, D_FF // FF_TILE),
        in_specs=[
            pl.BlockSpec((ROW_TILE, d), lambda i, f: (i, 0)),
            pl.BlockSpec((1, d), lambda i, f: (0, 0)),
            pl.BlockSpec((d, FF_TILE), lambda i, f: (0, f)),
            pl.BlockSpec((d, FF_TILE), lambda i, f: (0, f)),
            pl.BlockSpec((FF_TILE, d), lambda i, f: (f, 0)),
        ],
        out_specs=pl.BlockSpec((ROW_TILE, d), lambda i, f: (i, 0)),
        out_shape=jax.ShapeDtypeStruct((t, d), F32),
        scratch_shapes=[pltpu.VMEM((ROW_TILE, d), BF16)],
        compiler_params=_cparams(2),
        name="swiglu_ffn",
    )(h, g.reshape(1, d), w_gate, w_up, w_down)


def _final_norm_kernel(h_ref, g_ref, o_ref):
    o_ref[...] = _rms(h_ref[...], g_ref[...])


def _final_norm(h, g, batch, seq):
    d = h.shape[1]
    n_tok = seq - N_META
    return pl.pallas_call(
        _final_norm_kernel,
        grid=(batch, n_tok // FINAL_ROWS),
        in_specs=[pl.BlockSpec((pl.Element(FINAL_ROWS), pl.Element(d)),
                               lambda b, r: ((b * (seq // 8) + N_META // 8 + r * (FINAL_ROWS // 8)) * 8, 0)),
                  pl.BlockSpec((1, d), lambda b, r: (0, 0))],
        out_specs=pl.BlockSpec((None, FINAL_ROWS, d), lambda b, r: (b, r, 0)),
        out_shape=jax.ShapeDtypeStruct((batch, n_tok, d), F32),
        compiler_params=_cparams(2),
        name="final_rmsnorm",
    )(h, g.reshape(1, d))


def _rope_tables(n_tok, rot_dim):
    rows = n_tok // GRID_W
    row = jnp.concatenate([jnp.zeros((N_META,), F32), jnp.repeat(jnp.arange(rows, dtype=F32), GRID_W)])
    col = jnp.concatenate([jnp.zeros((N_META,), F32), jnp.tile(jnp.arange(GRID_W, dtype=F32), rows)])
    sub = rot_dim // 4
    inv = ROPE_THETA ** (-jnp.arange(sub, dtype=F32) / sub)
    ang_r = row[:, None] * inv[None, :]
    ang_c = col[:, None] * inv[None, :]
    cr, sr, cc, sc = jnp.cos(ang_r), jnp.sin(ang_r), jnp.cos(ang_c), jnp.sin(ang_c)
    zero = jnp.zeros_like(sr)
    l = row.shape[0]
    pad1 = jnp.ones((l, LANES - rot_dim), F32)
    pad0 = jnp.zeros((l, LANES - rot_dim), F32)
    cos = jnp.concatenate([cr, cr, cc, cc, pad1], axis=1)
    sin_lo = jnp.concatenate([-sr, zero, -sc, zero, pad0], axis=1)
    sin_hi = jnp.concatenate([zero, sr, zero, sc, pad0], axis=1)
    return cos, sin_lo, sin_hi


def _pack_layer_weights(w_in, gate_b, w_uq):
    o = 0
    a, o = w_in[:, o:o + C_CONV], o + C_CONV
    gte, o = w_in[:, o:o + C_CONV], o + C_CONV
    q, o = w_in[:, o:o + 1024], o + 1024
    k, o = w_in[:, o:o + 256], o + 256
    v, o = w_in[:, o:o + 256], o + 256
    cq, o = w_in[:, o:o + MLA_RANK], o + MLA_RANK
    ckv, o = w_in[:, o:o + MLA_RANK], o + MLA_RANK
    kpe, o = w_in[:, o:o + MLA_ROPE], o + MLA_ROPE
    gates = w_in[:, o:]
    half = IN_TILE // 2
    glu = [blk for t in range(GLU_TILES) for blk in (a[:, t * half:(t + 1) * half], gte[:, t * half:(t + 1) * half])]
    zeros = jnp.zeros((D_MODEL, A_COLS - A_END + LANES - MLA_ROPE), w_in.dtype)
    w_all = jnp.concatenate(glu + [gates, q, cq, ckv, k, v, kpe, zeros], axis=1).astype(BF16)
    n_glu = GLU_TILES * IN_TILE
    bias_all = jnp.concatenate([jnp.zeros((n_glu,), F32), gate_b.astype(F32), jnp.zeros((A_COLS,), F32)])
    wq = w_uq.reshape(MLA_RANK, MLA_HEADS, MLA_QK)
    wq = jnp.concatenate([wq, jnp.zeros((MLA_RANK, MLA_HEADS, 2 * LANES - MLA_QK), w_uq.dtype)], axis=2)
    wq = wq.reshape(MLA_RANK, MLA_HEADS * 2 * LANES)
    return w_all, bias_all.reshape(1, -1), wq.astype(BF16)


def kernel(x, meta_tokens, mix_norm_g, w_in, conv_dw, conv_b, conv_ln_g, conv_ln_b, w_conv_out, gqa_q_norm_g, gqa_k_norm_g, w_gqa_out, mla_q_norm_g, w_mla_uq, mla_kv_norm_g, w_mla_ukv, w_mla_out, gate_b, w_out, ffn_norm_g, w_ffn_gate, w_ffn_up, w_ffn_down, final_norm_g):
    batch, n_tok, d = x.shape
    seq = N_META + n_tok
    depth = w_in.shape[0]
    assert d == D_MODEL and seq == Q_TILE and seq % CONV_ROWS == 0
    assert sum(GQA_SUB_TILES) == Q_TILE and sum(MLA_SUB_TILES) == Q_TILE
    assert (batch * seq) % ROW_TILE == 0 and (batch * seq) % MERGE_ROW_TILE == 0

    meta = jnp.broadcast_to(meta_tokens.astype(x.dtype)[None], (batch, N_META, d))
    h = jnp.concatenate([meta, x], axis=1).reshape(batch * seq, d)

    tabs_gqa = _rope_tables(n_tok, HEAD_DIM)
    tabs_mla = _rope_tables(n_tok, MLA_ROPE)

    for i in range(depth):
        w_all, bias_all, w_uq = _pack_layer_weights(w_in[i], gate_b[i], w_mla_uq[i])
        z, gates, pa = _in_proj(h, mix_norm_g[i], w_all, bias_all)

        c = _depthwise_conv(z, conv_dw[i].reshape(CONV_K, C_CONV), conv_b[i], batch, seq)

        o_b = _gqa_attention(pa, tabs_gqa, gqa_q_norm_g[i], gqa_k_norm_g[i], batch, seq)

        q_mla = _rms_matmul(pa, mla_q_norm_g[i], w_uq, x_col_block=A_CQ // MLA_RANK, k_dim=MLA_RANK)
        kv_mla = _rms_matmul(pa, mla_kv_norm_g[i], w_mla_ukv[i].astype(BF16),
                             x_col_block=A_CKV // MLA_RANK, k_dim=MLA_RANK)
        o_c = _mla_attention(q_mla, kv_mla, pa, tabs_mla, batch, seq)

        h = _merge(c, o_b, o_c, gates, h, conv_ln_g[i], conv_ln_b[i],
                   w_conv_out[i].astype(BF16), w_gqa_out[i].astype(BF16), w_mla_out[i].astype(BF16),
                   w_out[i].astype(BF16))

        h = _ffn(h, ffn_norm_g[i], w_ffn_gate[i].astype(BF16), w_ffn_up[i].astype(BF16),
                 w_ffn_down[i].astype(BF16))

    return _final_norm(h, final_norm_g, batch, seq)
```

```python
import functools
import math

import jax
import jax.numpy as jnp
from jax import lax
from jax.experimental import pallas as pl
from jax.experimental.pallas import tpu as pltpu

D_MODEL = 2048
N_META = 16
GRID_W = 64
ROPE_THETA = 10000.0
NORM_EPS = 1e-6
C_CONV = D_MODEL // 2
CONV_K = 31
GQA_HEADS = 8
GQA_KV_HEADS = 2
GQA_GROUP = GQA_HEADS // GQA_KV_HEADS
HEAD_DIM = 128
MLA_HEADS = 8
MLA_RANK = D_MODEL // 4
MLA_NOPE = 128
MLA_ROPE = 64
MLA_QK = MLA_NOPE + MLA_ROPE
D_FF = 5632

LANES = 128
MXU_DIM = 256
VMEM_LIMIT = 56 * 1024 * 1024

ROW_TILE = 768
IN_ROW_TILE = 768
MERGE_ROW_TILE = 256
FF_TILE = 512
FINAL_ROWS = 512
PACK_ROWS = 256
Q_TILE = 2064
GQA_SUB_TILES = (176,) * 11 + (128,)
MLA_SUB_TILES = (416,) * 4 + (400,)
ATTN_RING = 4
ATTN_LOOKAHEAD = 2
LOG2E = 1.4426950408889634
CONV_ROWS = 48
CONV_PAD = 16

A_Q, A_CQ, A_CKV, A_K, A_V, A_KPE, A_END = 0, 1024, 1536, 2048, 2304, 2560, 2688
IN_TILE = 1024
A_COLS = 3 * IN_TILE
GLU_TILES = 2 * C_CONV // IN_TILE
GATE_TILES = 3 * D_MODEL // IN_TILE
ATTN_TILES = A_COLS // IN_TILE

F32 = jnp.float32
BF16 = jnp.bfloat16
NEG_BIG = -1e30


def _cparams(n_axes):
    return pltpu.CompilerParams(
        dimension_semantics=("arbitrary",) * n_axes, vmem_limit_bytes=VMEM_LIMIT)


def _rms(x, g):
    return x * lax.rsqrt(jnp.mean(x * x, axis=-1, keepdims=True) + NORM_EPS) * g


def _rope(x, cos, sin_lo, sin_hi, shift):
    up = pltpu.roll(x, LANES - shift, 1)
    dn = pltpu.roll(x, shift, 1)
    return x * cos + up * sin_lo + dn * sin_hi


def _dot_nt(a, b):
    return lax.dot_general(a, b, (((1,), (1,)), ((), ())), preferred_element_type=F32)


def _dot(a, b):
    return jnp.dot(a, b, preferred_element_type=F32)


def _mla_up_kernel(cq_ref, ckv_ref, gq_ref, gkv_ref, wq_ref, wkv_ref, q_ref, kv_ref):
    q_ref[...] = _dot(_rms(cq_ref[...].astype(F32), gq_ref[...]).astype(BF16), wq_ref[...]).astype(q_ref.dtype)
    kv_ref[...] = _dot(_rms(ckv_ref[...].astype(F32), gkv_ref[...]).astype(BF16), wkv_ref[...]).astype(kv_ref.dtype)


def _mla_up(pa, g_q, g_kv, w_uq, w_ukv, layer):
    t = pa.shape[0]
    n = w_uq.shape[2]
    x_spec = lambda off: pl.BlockSpec((ROW_TILE, MLA_RANK), lambda i: (i, off // MLA_RANK))
    g_spec = pl.BlockSpec((1, MLA_RANK), lambda i: (0, 0))
    w_spec = pl.BlockSpec((None, MLA_RANK, n), lambda i: (layer, 0, 0))
    o_spec = pl.BlockSpec((ROW_TILE, n), lambda i: (i, 0))
    return pl.pallas_call(
        _mla_up_kernel,
        grid=(t // ROW_TILE,),
        in_specs=[x_spec(A_CQ), x_spec(A_CKV), g_spec, g_spec, w_spec, w_spec],
        out_specs=[o_spec, o_spec],
        out_shape=[jax.ShapeDtypeStruct((t, n), BF16), jax.ShapeDtypeStruct((t, n), BF16)],
        compiler_params=_cparams(1),
        name="mla_up_proj",
    )(pa, pa, g_q.reshape(1, MLA_RANK), g_kv.reshape(1, MLA_RANK), w_uq, w_ukv)


def _in_proj_kernel(x_ref, g_ref, w_ref, b_ref, z_ref, gates_ref, pa_ref, xn_ref):
    j = pl.program_id(1)

    @pl.when(j == 0)
    def _():
        xn_ref[...] = _rms(x_ref[...], g_ref[...]).astype(BF16)

    @pl.when(j < GLU_TILES)
    def _():
        acc = _dot(xn_ref[...], w_ref[...])
        half = IN_TILE // 2
        z_ref[...] = (acc[:, :half] * jax.nn.sigmoid(acc[:, half:])).astype(z_ref.dtype)

    @pl.when((j >= GLU_TILES) & (j < GLU_TILES + GATE_TILES))
    def _():
        acc = _dot(xn_ref[...], w_ref[...])
        gates_ref[...] = jax.nn.sigmoid(acc + b_ref[...]).astype(gates_ref.dtype)

    @pl.when(j >= GLU_TILES + GATE_TILES)
    def _():
        pa_ref[...] = _dot(xn_ref[...], w_ref[...]).astype(pa_ref.dtype)


def _in_proj(h, g, w_all, bias_all, layer):
    t, d = h.shape
    n_tiles = GLU_TILES + GATE_TILES + ATTN_TILES
    clamp = lambda j, lo, n: jnp.clip(j - lo, 0, n - 1)
    bm = IN_ROW_TILE
    return pl.pallas_call(
        _in_proj_kernel,
        grid=(t // bm, n_tiles),
        in_specs=[
            pl.BlockSpec((bm, d), lambda i, j: (i, 0)),
            pl.BlockSpec((1, d), lambda i, j: (0, 0)),
            pl.BlockSpec((None, d, IN_TILE), lambda i, j: (layer, 0, j)),
            pl.BlockSpec((None, 1, IN_TILE), lambda i, j: (layer, 0, j)),
        ],
        out_specs=[
            pl.BlockSpec((bm, IN_TILE // 2), lambda i, j: (i, clamp(j, 0, GLU_TILES))),
            pl.BlockSpec((bm, IN_TILE), lambda i, j: (i, clamp(j, GLU_TILES, GATE_TILES))),
            pl.BlockSpec((bm, IN_TILE), lambda i, j: (i, clamp(j, GLU_TILES + GATE_TILES, ATTN_TILES))),
        ],
        out_shape=[jax.ShapeDtypeStruct((t, C_CONV), BF16),
                   jax.ShapeDtypeStruct((t, 3 * D_MODEL), BF16),
                   jax.ShapeDtypeStruct((t, A_COLS), BF16)],
        scratch_shapes=[pltpu.VMEM((bm, d), BF16)],
        compiler_params=_cparams(2),
        name="in_proj",
    )(h, g.reshape(1, d), w_all, bias_all)


def _conv_kernel(z_ref, dw_ref, cb_ref, o_ref, zp_ref, *, seq):
    tail = zp_ref.shape[0] - CONV_PAD - seq
    zp_ref[0:CONV_PAD, :] = jnp.zeros((CONV_PAD, LANES), F32)
    zp_ref[CONV_PAD + seq:, :] = jnp.zeros((tail, LANES), F32)
    zp_ref[CONV_PAD:CONV_PAD + seq, :] = z_ref[...].astype(F32)
    bias = cb_ref[...]

    def chunk(c, carry):
        base = pl.multiple_of(c * CONV_ROWS, 8)
        acc = jnp.zeros((CONV_ROWS, LANES), F32) + bias
        for j in range(CONV_K):
            off = CONV_PAD - CONV_K // 2 + j
            acc = acc + zp_ref[pl.ds(base + off, CONV_ROWS), :] * dw_ref[j:j + 1, :]
        o_ref[pl.ds(base, CONV_ROWS), :] = acc.astype(o_ref.dtype)
        return carry

    lax.fori_loop(0, seq // CONV_ROWS, chunk, 0)


def _depthwise_conv(z, dw, cb, batch, seq):
    c = z.shape[1]
    z3 = z.reshape(batch, seq, c)
    pad_rows = CONV_PAD + seq + CONV_PAD
    out = pl.pallas_call(
        functools.partial(_conv_kernel, seq=seq),
        grid=(batch, c // LANES),
        in_specs=[
            pl.BlockSpec((None, seq, LANES), lambda b, j: (b, 0, j)),
            pl.BlockSpec((CONV_K, LANES), lambda b, j: (0, j)),
            pl.BlockSpec((1, LANES), lambda b, j: (0, j)),
        ],
        out_specs=pl.BlockSpec((None, seq, LANES), lambda b, j: (b, 0, j)),
        out_shape=jax.ShapeDtypeStruct((batch, seq, c), BF16),
        scratch_shapes=[pltpu.VMEM((pad_rows, LANES), F32)],
        compiler_params=_cparams(2),
        name="depthwise_conv",
    )(z3, dw, cb.reshape(1, c))
    return out.reshape(batch * seq, c)


def _softmax_pv(q, k_s, v_s, s_ref, p_ref, o_ref, n_real, n_meta, sub_tiles):
    lane = lax.broadcasted_iota(jnp.int32, (1, LANES), 1)
    tail_bias = jnp.where(lane < n_meta, 0.0, NEG_BIG).astype(F32)
    slot_rows = max(sub_tiles)
    starts = [sum(sub_tiles[:i]) for i in range(len(sub_tiles))]

    def scores(idx):
        r0, rows, b0 = starts[idx], sub_tiles[idx], (idx % ATTN_RING) * slot_rows
        qs = q[r0:r0 + rows, :]
        s_ref[b0:b0 + rows, 0:n_real] = _dot_nt(qs, k_s[0:n_real, :])
        s_ref[b0:b0 + rows, n_real:n_real + LANES] = _dot_nt(qs, k_s[n_real:n_real + LANES, :]) + tail_bias

    def finish(idx):
        r0, rows, b0 = starts[idx], sub_tiles[idx], (idx % ATTN_RING) * slot_rows
        s = s_ref[b0:b0 + rows, :]
        m = jnp.max(s, axis=-1, keepdims=True)
        p_ref[b0:b0 + rows, :] = jnp.exp2(s - m).astype(BF16)
        o_ext = _dot(p_ref[b0:b0 + rows, :], v_s[...])
        o_ref[r0:r0 + rows, :] = (o_ext[:, 0:LANES] / o_ext[:, LANES:2 * LANES]).astype(o_ref.dtype)

    n_sub = len(sub_tiles)
    for idx in range(min(ATTN_LOOKAHEAD, n_sub)):
        scores(idx)
    for idx in range(n_sub):
        if idx + ATTN_LOOKAHEAD < n_sub:
            scores(idx + ATTN_LOOKAHEAD)
        finish(idx)


def _fill_keys(dst_ref, col, make_rows, n_real, n_meta):
    dst_ref[0:n_real, col:col + LANES] = make_rows(n_meta, n_real)
    dst_ref[n_real:n_real + n_meta, col:col + LANES] = make_rows(0, n_meta)
    dst_ref[n_real + n_meta:n_real + LANES, col:col + LANES] = jnp.zeros((LANES - n_meta, LANES), BF16)


def _fill_ones(v_s):
    v_s[:, LANES:2 * LANES] = jnp.ones((v_s.shape[0], LANES), BF16)


def _gqa_kernel(q_ref, k_ref, v_ref, cos_ref, lo_ref, hi_ref, qg_ref, kg_ref, o_ref, k_s, v_s, s_ref, p_ref, *, n_real, n_meta):
    head = pl.program_id(1)
    shift = HEAD_DIM // 4

    @pl.when(lax.rem(head, GQA_GROUP) == 0)
    def _():
        def k_rows(start, size):
            x = _rms(k_ref[start:start + size, :].astype(F32), kg_ref[...])
            x = _rope(x, cos_ref[start:start + size, :], lo_ref[start:start + size, :],
                      hi_ref[start:start + size, :], shift)
            return x.astype(BF16)

        def v_rows(start, size):
            return v_ref[start:start + size, :]

        _fill_keys(k_s, 0, k_rows, n_real, n_meta)
        _fill_keys(v_s, 0, v_rows, n_real, n_meta)
        _fill_ones(v_s)

    q = _rms(q_ref[...].astype(F32), qg_ref[...])
    q = _rope(q, cos_ref[...], lo_ref[...], hi_ref[...], shift)
    q = (q * (LOG2E / math.sqrt(HEAD_DIM))).astype(BF16)
    _softmax_pv(q, k_s, v_s, s_ref, p_ref, o_ref, n_real, n_meta, GQA_SUB_TILES)


def _gqa_attention(pa, tabs, qn_g, kn_g, batch, seq):
    pa3 = pa.reshape(batch, seq, A_COLS)
    n_real = seq - N_META
    q_blk, k_blk, v_blk = A_Q // LANES, A_K // LANES, A_V // LANES
    tab = pl.BlockSpec((seq, LANES), lambda b, h: (0, 0))
    gspec = pl.BlockSpec((1, LANES), lambda b, h: (0, 0))
    out = pl.pallas_call(
        functools.partial(_gqa_kernel, n_real=n_real, n_meta=N_META),
        grid=(batch, GQA_HEADS),
        in_specs=[
            pl.BlockSpec((None, seq, LANES), lambda b, h: (b, 0, q_blk + h)),
            pl.BlockSpec((None, seq, LANES), lambda b, h: (b, 0, k_blk + h // GQA_GROUP)),
            pl.BlockSpec((None, seq, LANES), lambda b, h: (b, 0, v_blk + h // GQA_GROUP)),
            tab, tab, tab, gspec, gspec,
        ],
        out_specs=pl.BlockSpec((None, seq, LANES), lambda b, h: (b, 0, h)),
        out_shape=jax.ShapeDtypeStruct((batch, seq, GQA_HEADS * HEAD_DIM), BF16),
        scratch_shapes=[pltpu.VMEM((n_real + LANES, HEAD_DIM), BF16),
                        pltpu.VMEM((n_real + LANES, 2 * LANES), BF16),
                        pltpu.VMEM((ATTN_RING * max(GQA_SUB_TILES), n_real + LANES), F32),
                        pltpu.VMEM((ATTN_RING * max(GQA_SUB_TILES), n_real + LANES), BF16)],
        compiler_params=_cparams(2),
        name="gqa_attention",
    )(pa3, pa3, pa3, tabs[0], tabs[1], tabs[2],
      qn_g.reshape(1, HEAD_DIM), kn_g.reshape(1, HEAD_DIM))
    return out.reshape(batch * seq, GQA_HEADS * HEAD_DIM)


def _mla_kernel(q_ref, kv_ref, kpe_ref, cos_ref, lo_ref, hi_ref, o_ref, k_s, v_s, s_ref, p_ref, *, n_real, n_meta):
    shift = MLA_ROPE // 4

    def knope_rows(start, size):
        return kv_ref[start:start + size, 0:LANES]

    def kpe_rows(start, size):
        x = _rope(kpe_ref[start:start + size, :].astype(F32), cos_ref[start:start + size, :],
                  lo_ref[start:start + size, :], hi_ref[start:start + size, :], shift)
        return x.astype(BF16)

    def v_rows(start, size):
        return kv_ref[start:start + size, LANES:2 * LANES]

    @pl.when(pl.program_id(1) == 0)
    def _():
        _fill_keys(k_s, LANES, kpe_rows, n_real, n_meta)
        _fill_ones(v_s)

    _fill_keys(k_s, 0, knope_rows, n_real, n_meta)
    _fill_keys(v_s, 0, v_rows, n_real, n_meta)

    scale = LOG2E / math.sqrt(MLA_QK)
    q_nope = q_ref[:, 0:LANES].astype(F32) * scale
    q_pe = _rope(q_ref[:, LANES:2 * LANES].astype(F32), cos_ref[...], lo_ref[...], hi_ref[...], shift) * scale
    q = jnp.concatenate([q_nope, q_pe], axis=-1).astype(BF16)
    _softmax_pv(q, k_s, v_s, s_ref, p_ref, o_ref, n_real, n_meta, MLA_SUB_TILES)


def _mla_attention(q_mla, kv_mla, pa, tabs, batch, seq):
    n_real = seq - N_META
    width = 2 * LANES
    q3 = q_mla.reshape(batch, seq, MLA_HEADS * width)
    kv3 = kv_mla.reshape(batch, seq, MLA_HEADS * width)
    pa3 = pa.reshape(batch, seq, A_COLS)
    kpe_blk = A_KPE // LANES
    tab = pl.BlockSpec((seq, LANES), lambda b, h: (0, 0))
    out = pl.pallas_call(
        functools.partial(_mla_kernel, n_real=n_real, n_meta=N_META),
        grid=(batch, MLA_HEADS),
        in_specs=[
            pl.BlockSpec((None, seq, width), lambda b, h: (b, 0, h)),
            pl.BlockSpec((None, seq, width), lambda b, h: (b, 0, h)),
            pl.BlockSpec((None, seq, LANES), lambda b, h: (b, 0, kpe_blk)),
            tab, tab, tab,
        ],
        out_specs=pl.BlockSpec((None, seq, LANES), lambda b, h: (b, 0, h)),
        out_shape=jax.ShapeDtypeStruct((batch, seq, MLA_HEADS * LANES), BF16),
        scratch_shapes=[pltpu.VMEM((n_real + LANES, width), BF16),
                        pltpu.VMEM((n_real + LANES, 2 * LANES), BF16),
                        pltpu.VMEM((ATTN_RING * max(MLA_SUB_TILES), n_real + LANES), F32),
                        pltpu.VMEM((ATTN_RING * max(MLA_SUB_TILES), n_real + LANES), BF16)],
        compiler_params=_cparams(2),
        name="mla_attention",
    )(q3, kv3, pa3, tabs[0], tabs[1], tabs[2])
    return out.reshape(batch * seq, MLA_HEADS * LANES)


def _merge_kernel(c_ref, ob_ref, oc_ref, g0_ref, g1_ref, g2_ref, h_ref, lng_ref, lnb_ref,
                  wa_ref, wb_ref, wc_ref, wo_ref, o_ref):
    c = c_ref[...].astype(F32)
    mu = jnp.mean(c, axis=-1, keepdims=True)
    xc = c - mu
    y = xc * lax.rsqrt(jnp.mean(xc * xc, axis=-1, keepdims=True) + NORM_EPS)
    y = y * lng_ref[...] + lnb_ref[...]
    act = (y * jax.nn.sigmoid(y)).astype(BF16)
    merged = g0_ref[...].astype(F32) * _dot(act, wa_ref[...])
    merged = merged + g1_ref[...].astype(F32) * _dot(ob_ref[...], wb_ref[...])
    merged = merged + g2_ref[...].astype(F32) * _dot(oc_ref[...], wc_ref[...])
    o_ref[...] = h_ref[...] + _dot(merged.astype(BF16), wo_ref[...])


def _merge(c, o_b, o_c, gates, h, ln_g, ln_b, w_a, w_b, w_c, w_o, layer):
    t, d = h.shape
    bm = MERGE_ROW_TILE
    row = lambda width, col: pl.BlockSpec((bm, width), lambda i: (i, col))
    const = lambda shape: pl.BlockSpec(shape, lambda i: (0, 0), pipeline_mode=pl.Buffered(1))
    weight = lambda k: pl.BlockSpec((None, k, d), lambda i: (layer, 0, 0), pipeline_mode=pl.Buffered(1))
    return pl.pallas_call(
        _merge_kernel,
        grid=(t // bm,),
        in_specs=[row(C_CONV, 0), row(C_CONV, 0), row(C_CONV, 0),
                  row(d, 0), row(d, 1), row(d, 2), row(d, 0),
                  const((1, C_CONV)), const((1, C_CONV)),
                  weight(C_CONV), weight(C_CONV), weight(C_CONV), weight(d)],
        out_specs=row(d, 0),
        out_shape=jax.ShapeDtypeStruct((t, d), F32),
        compiler_params=_cparams(1),
        name="branch_merge",
    )(c, o_b, o_c, gates, gates, gates, h, ln_g.reshape(1, C_CONV), ln_b.reshape(1, C_CONV),
      w_a, w_b, w_c, w_o)


def _ffn_kernel(h_ref, g_ref, wg_ref, wu_ref, wd_ref, o_ref, v_ref):
    @pl.when(pl.program_id(1) == 0)
    def _():
        hv = h_ref[...]
        v_ref[...] = _rms(hv, g_ref[...]).astype(BF16)
        o_ref[...] = hv

    v = v_ref[...]
    gate = _dot(v, wg_ref[...])
    up = _dot(v, wu_ref[...])
    a = (gate * jax.nn.sigmoid(gate) * up).astype(BF16)
    o_ref[...] += _dot(a, wd_ref[...])


def _ffn(h, g, w_gate, w_up, w_down, layer):
    t, d = h.shape
    return pl.pallas_call(
        _ffn_kernel,
        grid=(t // ROW_TILE, D_FF // FF_TILE),
        in_specs=[
            pl.BlockSpec((ROW_TILE, d), lambda i, f: (i, 0)),
            pl.BlockSpec((1, d), lambda i, f: (0, 0)),
            pl.BlockSpec((None, d, FF_TILE), lambda i, f: (layer, 0, f)),
            pl.BlockSpec((None, d, FF_TILE), lambda i, f: (layer, 0, f)),
            pl.BlockSpec((None, FF_TILE, d), lambda i, f: (layer, f, 0)),
        ],
        out_specs=pl.BlockSpec((ROW_TILE, d), lambda i, f: (i, 0)),
        out_shape=jax.ShapeDtypeStruct((t, d), F32),
        scratch_shapes=[pltpu.VMEM((ROW_TILE, d), BF16)],
        compiler_params=_cparams(2),
        name="swiglu_ffn",
    )(h, g.reshape(1, d), w_gate, w_up, w_down)


def _final_norm_kernel(h_ref, g_ref, o_ref):
    o_ref[...] = _rms(h_ref[...], g_ref[...])


def _final_norm(h, g, batch, seq):
    d = h.shape[1]
    n_tok = seq - N_META
    return pl.pallas_call(
        _final_norm_kernel,
        grid=(batch, n_tok // FINAL_ROWS),
        in_specs=[pl.BlockSpec((pl.Element(FINAL_ROWS), pl.Element(d)),
                               lambda b, r: ((b * (seq // 8) + N_META // 8 + r * (FINAL_ROWS // 8)) * 8, 0)),
                  pl.BlockSpec((1, d), lambda b, r: (0, 0))],
        out_specs=pl.BlockSpec((None, FINAL_ROWS, d), lambda b, r: (b, r, 0)),
        out_shape=jax.ShapeDtypeStruct((batch, n_tok, d), F32),
        compiler_params=_cparams(2),
        name="final_rmsnorm",
    )(h, g.reshape(1, d))


def _rope_tables(n_tok, rot_dim):
    rows = n_tok // GRID_W
    row = jnp.concatenate([jnp.zeros((N_META,), F32), jnp.repeat(jnp.arange(rows, dtype=F32), GRID_W)])
    col = jnp.concatenate([jnp.zeros((N_META,), F32), jnp.tile(jnp.arange(GRID_W, dtype=F32), rows)])
    sub = rot_dim // 4
    inv = ROPE_THETA ** (-jnp.arange(sub, dtype=F32) / sub)
    ang_r = row[:, None] * inv[None, :]
    ang_c = col[:, None] * inv[None, :]
    cr, sr, cc, sc = jnp.cos(ang_r), jnp.sin(ang_r), jnp.cos(ang_c), jnp.sin(ang_c)
    zero = jnp.zeros_like(sr)
    l = row.shape[0]
    pad1 = jnp.ones((l, LANES - rot_dim), F32)
    pad0 = jnp.zeros((l, LANES - rot_dim), F32)
    cos = jnp.concatenate([cr, cr, cc, cc, pad1], axis=1)
    sin_lo = jnp.concatenate([-sr, zero, -sc, zero, pad0], axis=1)
    sin_hi = jnp.concatenate([zero, sr, zero, sc, pad0], axis=1)
    return cos, sin_lo, sin_hi


def _in_proj_segments():
    src = {}
    o = 0
    for name, width in (("a", C_CONV), ("gte", C_CONV), ("q", 1024), ("k", 256), ("v", 256),
                        ("cq", MLA_RANK), ("ckv", MLA_RANK), ("kpe", MLA_ROPE), ("gates", 3 * D_MODEL)):
        src[name] = o
        o += width
    half = IN_TILE // 2
    segs = []
    for t in range(GLU_TILES):
        segs.append((src["a"] + t * half, t * IN_TILE, half))
        segs.append((src["gte"] + t * half, t * IN_TILE + half, half))
    base = GLU_TILES * IN_TILE
    segs.append((src["gates"], base, 3 * D_MODEL))
    base += 3 * D_MODEL
    for name, off, width in (("q", A_Q, 1024), ("cq", A_CQ, MLA_RANK), ("ckv", A_CKV, MLA_RANK),
                             ("k", A_K, 256), ("v", A_V, 256), ("kpe", A_KPE, MLA_ROPE)):
        segs.append((src[name], base + off, width))
    return segs, base + A_KPE + MLA_ROPE, base + A_COLS


def _pack_w_in_kernel(x_ref, o_ref):
    segs, zero_from, total = _in_proj_segments()
    for s, d, w in segs:
        o_ref[:, d:d + w] = x_ref[:, s:s + w].astype(o_ref.dtype)
    o_ref[:, zero_from:total] = jnp.zeros((o_ref.shape[0], total - zero_from), o_ref.dtype)


def _pack_w_in(w_in):
    depth, d, n_in = w_in.shape
    total = _in_proj_segments()[2]
    return pl.pallas_call(
        _pack_w_in_kernel,
        grid=(depth, d // PACK_ROWS),
        in_specs=[pl.BlockSpec((None, PACK_ROWS, n_in), lambda a, r: (a, r, 0))],
        out_specs=pl.BlockSpec((None, PACK_ROWS, total), lambda a, r: (a, r, 0)),
        out_shape=jax.ShapeDtypeStruct((depth, d, total), BF16),
        compiler_params=_cparams(2),
        name="pack_w_in",
    )(w_in)


def _cast_kernel(x_ref, o_ref):
    o_ref[...] = x_ref[...].astype(o_ref.dtype)


def _to_bf16(w):
    depth, k, n = w.shape
    rows = min(k, PACK_ROWS)
    return pl.pallas_call(
        _cast_kernel,
        grid=(depth, k // rows),
        in_specs=[pl.BlockSpec((None, rows, n), lambda a, r: (a, r, 0))],
        out_specs=pl.BlockSpec((None, rows, n), lambda a, r: (a, r, 0)),
        out_shape=jax.ShapeDtypeStruct(w.shape, BF16),
        compiler_params=_cparams(2),
        name="cast_bf16",
    )(w)


def _pad_uq(w_uq):
    depth = w_uq.shape[0]
    wq = w_uq.reshape(depth, MLA_RANK, MLA_HEADS, MLA_QK)
    wq = jnp.concatenate([wq, jnp.zeros((depth, MLA_RANK, MLA_HEADS, 2 * LANES - MLA_QK), w_uq.dtype)], axis=3)
    return wq.reshape(depth, MLA_RANK, MLA_HEADS * 2 * LANES).astype(BF16)


def _in_proj_bias(gate_b):
    depth = gate_b.shape[0]
    n_glu = GLU_TILES * IN_TILE
    return jnp.concatenate([jnp.zeros((depth, 1, n_glu), F32), gate_b.astype(F32)[:, None, :],
                            jnp.zeros((depth, 1, A_COLS), F32)], axis=2)


def kernel(x, meta_tokens, mix_norm_g, w_in, conv_dw, conv_b, conv_ln_g, conv_ln_b, w_conv_out, gqa_q_norm_g, gqa_k_norm_g, w_gqa_out, mla_q_norm_g, w_mla_uq, mla_kv_norm_g, w_mla_ukv, w_mla_out, gate_b, w_out, ffn_norm_g, w_ffn_gate, w_ffn_up, w_ffn_down, final_norm_g):
    batch, n_tok, d = x.shape
    seq = N_META + n_tok
    depth = w_in.shape[0]
    assert d == D_MODEL and seq == Q_TILE and seq % CONV_ROWS == 0
    assert sum(GQA_SUB_TILES) == Q_TILE and sum(MLA_SUB_TILES) == Q_TILE
    assert all((batch * seq) % tile == 0 for tile in (ROW_TILE, IN_ROW_TILE, MERGE_ROW_TILE))

    meta = jnp.broadcast_to(meta_tokens.astype(x.dtype)[None], (batch, N_META, d))
    h = jnp.concatenate([meta, x], axis=1).reshape(batch * seq, d)

    tabs_gqa = _rope_tables(n_tok, HEAD_DIM)
    tabs_mla = _rope_tables(n_tok, MLA_ROPE)

    w_all = _pack_w_in(w_in)
    bias_all = _in_proj_bias(gate_b)
    w_uq = _pad_uq(w_mla_uq)
    w_ukv, w_a, w_b, w_c, w_o = (_to_bf16(w) for w in (w_mla_ukv, w_conv_out, w_gqa_out, w_mla_out, w_out))
    w_fg, w_fu, w_fd = (_to_bf16(w) for w in (w_ffn_gate, w_ffn_up, w_ffn_down))

    for i in range(depth):
        z, gates, pa = _in_proj(h, mix_norm_g[i], w_all, bias_all, i)

        c = _depthwise_conv(z, conv_dw[i].reshape(CONV_K, C_CONV), conv_b[i], batch, seq)

        o_b = _gqa_attention(pa, tabs_gqa, gqa_q_norm_g[i], gqa_k_norm_g[i], batch, seq)

        q_mla, kv_mla = _mla_up(pa, mla_q_norm_g[i], mla_kv_norm_g[i], w_uq, w_ukv, i)
        o_c = _mla_attention(q_mla, kv_mla, pa, tabs_mla, batch, seq)

        h = _merge(c, o_b, o_c, gates, h, conv_ln_g[i], conv_ln_b[i], w_a, w_b, w_c, w_o, i)

        h = _ffn(h, ffn_norm_g[i], w_fg, w_fu, w_fd, i)

    return _final_norm(h, final_norm_g, batch, seq)
```

```python
import functools
import math

import jax
import jax.numpy as jnp
from jax import lax
from jax.experimental import pallas as pl
from jax.experimental.pallas import tpu as pltpu

D_MODEL = 2048
N_META = 16
GRID_W = 64
ROPE_THETA = 10000.0
NORM_EPS = 1e-6
C_CONV = D_MODEL // 2
CONV_K = 31
GQA_HEADS = 8
GQA_KV_HEADS = 2
GQA_GROUP = GQA_HEADS // GQA_KV_HEADS
HEAD_DIM = 128
MLA_HEADS = 8
MLA_RANK = D_MODEL // 4
MLA_NOPE = 128
MLA_ROPE = 64
MLA_QK = MLA_NOPE + MLA_ROPE
D_FF = 5632

LANES = 128
MXU_DIM = 256
VMEM_LIMIT = 56 * 1024 * 1024

ROW_TILE = 768
IN_ROW_TILE = 768
MERGE_ROW_TILE = 256
FF_TILE = 512
FINAL_ROWS = 512
PACK_ROWS = 256
Q_TILE = 2064
GQA_SUB_TILES = (176,) * 11 + (128,)
MLA_SUB_TILES = (416,) * 4 + (400,)
ATTN_RING = 4
ATTN_LOOKAHEAD = 2
SIDE_SLACK = 0
LOG2E = 1.4426950408889634
CONV_ROWS = 48
CONV_PAD = 16

A_Q, A_CQ, A_CKV, A_K, A_V, A_KPE, A_END = 0, 1024, 1536, 2048, 2304, 2560, 2688
IN_TILE = 1024
A_COLS = 3 * IN_TILE
GLU_TILES = 2 * C_CONV // IN_TILE
GATE_TILES = 3 * D_MODEL // IN_TILE
ATTN_TILES = A_COLS // IN_TILE

F32 = jnp.float32
BF16 = jnp.bfloat16
NEG_BIG = -1e30


def _cparams(n_axes):
    return pltpu.CompilerParams(
        dimension_semantics=("arbitrary",) * n_axes, vmem_limit_bytes=VMEM_LIMIT)


def _rms(x, g):
    return x * lax.rsqrt(jnp.mean(x * x, axis=-1, keepdims=True) + NORM_EPS) * g


def _rope(x, cos, sin_lo, sin_hi, shift):
    up = pltpu.roll(x, LANES - shift, 1)
    dn = pltpu.roll(x, shift, 1)
    return x * cos + up * sin_lo + dn * sin_hi


def _dot_nt(a, b):
    return lax.dot_general(a, b, (((1,), (1,)), ((), ())), preferred_element_type=F32)


def _dot(a, b):
    return jnp.dot(a, b, preferred_element_type=F32)


def _mla_up_kernel(cq_ref, ckv_ref, gq_ref, gkv_ref, wq_ref, wkv_ref, q_ref, kv_ref):
    q_ref[...] = _dot(_rms(cq_ref[...].astype(F32), gq_ref[...]).astype(BF16), wq_ref[...]).astype(q_ref.dtype)
    kv_ref[...] = _dot(_rms(ckv_ref[...].astype(F32), gkv_ref[...]).astype(BF16), wkv_ref[...]).astype(kv_ref.dtype)


def _mla_up(pa, g_q, g_kv, w_uq, w_ukv, layer):
    t = pa.shape[0]
    n = w_uq.shape[2]
    x_spec = lambda off: pl.BlockSpec((ROW_TILE, MLA_RANK), lambda i: (i, off // MLA_RANK))
    g_spec = pl.BlockSpec((1, MLA_RANK), lambda i: (0, 0))
    w_spec = pl.BlockSpec((None, MLA_RANK, n), lambda i: (layer, 0, 0))
    o_spec = pl.BlockSpec((ROW_TILE, n), lambda i: (i, 0))
    return pl.pallas_call(
        _mla_up_kernel,
        grid=(t // ROW_TILE,),
        in_specs=[x_spec(A_CQ), x_spec(A_CKV), g_spec, g_spec, w_spec, w_spec],
        out_specs=[o_spec, o_spec],
        out_shape=[jax.ShapeDtypeStruct((t, n), BF16), jax.ShapeDtypeStruct((t, n), BF16)],
        compiler_params=_cparams(1),
        name="mla_up_proj",
    )(pa, pa, g_q.reshape(1, MLA_RANK), g_kv.reshape(1, MLA_RANK), w_uq, w_ukv)


def _in_proj_kernel(x_ref, g_ref, w_ref, b_ref, z_ref, gates_ref, pa_ref, xn_ref):
    j = pl.program_id(1)

    @pl.when(j == 0)
    def _():
        xn_ref[...] = _rms(x_ref[...], g_ref[...]).astype(BF16)

    @pl.when(j < GLU_TILES)
    def _():
        acc = _dot(xn_ref[...], w_ref[...])
        half = IN_TILE // 2
        z_ref[...] = (acc[:, :half] * jax.nn.sigmoid(acc[:, half:])).astype(z_ref.dtype)

    @pl.when((j >= GLU_TILES) & (j < GLU_TILES + GATE_TILES))
    def _():
        acc = _dot(xn_ref[...], w_ref[...])
        gates_ref[...] = jax.nn.sigmoid(acc + b_ref[...]).astype(gates_ref.dtype)

    @pl.when(j >= GLU_TILES + GATE_TILES)
    def _():
        pa_ref[...] = _dot(xn_ref[...], w_ref[...]).astype(pa_ref.dtype)


def _in_proj(h, g, w_all, bias_all, layer):
    t, d = h.shape
    n_tiles = GLU_TILES + GATE_TILES + ATTN_TILES
    clamp = lambda j, lo, n: jnp.clip(j - lo, 0, n - 1)
    bm = IN_ROW_TILE
    return pl.pallas_call(
        _in_proj_kernel,
        grid=(t // bm, n_tiles),
        in_specs=[
            pl.BlockSpec((bm, d), lambda i, j: (i, 0)),
            pl.BlockSpec((1, d), lambda i, j: (0, 0)),
            pl.BlockSpec((None, d, IN_TILE), lambda i, j: (layer, 0, j)),
            pl.BlockSpec((None, 1, IN_TILE), lambda i, j: (layer, 0, j)),
        ],
        out_specs=[
            pl.BlockSpec((bm, IN_TILE // 2), lambda i, j: (i, clamp(j, 0, GLU_TILES))),
            pl.BlockSpec((bm, IN_TILE), lambda i, j: (i, clamp(j, GLU_TILES, GATE_TILES))),
            pl.BlockSpec((bm, IN_TILE), lambda i, j: (i, clamp(j, GLU_TILES + GATE_TILES, ATTN_TILES))),
        ],
        out_shape=[jax.ShapeDtypeStruct((t, C_CONV), BF16),
                   jax.ShapeDtypeStruct((t, 3 * D_MODEL), BF16),
                   jax.ShapeDtypeStruct((t, A_COLS), BF16)],
        scratch_shapes=[pltpu.VMEM((bm, d), BF16)],
        compiler_params=_cparams(2),
        name="in_proj",
    )(h, g.reshape(1, d), w_all, bias_all)


def _conv_stage(z_ref, zp_ref, seq):
    tail = zp_ref.shape[0] - CONV_PAD - seq
    zp_ref[0:CONV_PAD, :] = jnp.zeros((CONV_PAD, LANES), F32)
    zp_ref[CONV_PAD + seq:, :] = jnp.zeros((tail, LANES), F32)
    zp_ref[CONV_PAD:CONV_PAD + seq, :] = z_ref[...].astype(F32)


def _conv_chunk_thunks(zp_ref, dw_ref, cb_ref, o_ref, seq):
    def make(base):
        def thunk():
            acc = jnp.zeros((CONV_ROWS, LANES), F32) + cb_ref[...]
            for j in range(CONV_K):
                off = CONV_PAD - CONV_K // 2 + j
                acc = acc + zp_ref[base + off:base + off + CONV_ROWS, :] * dw_ref[j:j + 1, :]
            o_ref[base:base + CONV_ROWS, :] = acc.astype(o_ref.dtype)
            bits = pltpu.bitcast(acc[0:8, :], jnp.uint32)
            zero = lax.shift_right_logical(lax.shift_right_logical(bits, jnp.uint32(16)), jnp.uint32(16))
            return pltpu.bitcast(zero, F32)[0:1, :]
        return thunk
    return [make(c * CONV_ROWS) for c in range(seq // CONV_ROWS)]


def _softmax_pv(q, k_s, v_s, s_ref, p_ref, o_ref, n_real, n_meta, sub_tiles, side_work=()):
    lane = lax.broadcasted_iota(jnp.int32, (1, LANES), 1)
    tail_bias = jnp.where(lane < n_meta, 0.0, NEG_BIG).astype(F32)
    slot_rows = max(sub_tiles)
    starts = [sum(sub_tiles[:i]) for i in range(len(sub_tiles))]

    ordering = [jnp.zeros((1, LANES), F32)]

    def scores(idx):
        r0, rows, b0 = starts[idx], sub_tiles[idx], (idx % ATTN_RING) * slot_rows
        qs = q[r0:r0 + rows, :]
        s_ref[b0:b0 + rows, 0:n_real] = _dot_nt(qs, k_s[0:n_real, :])
        s_ref[b0:b0 + rows, n_real:n_real + LANES] = (
            _dot_nt(qs, k_s[n_real:n_real + LANES, :]) + (tail_bias + ordering[0]))

    def finish(idx):
        r0, rows, b0 = starts[idx], sub_tiles[idx], (idx % ATTN_RING) * slot_rows
        s = s_ref[b0:b0 + rows, :]
        m = jnp.max(s, axis=-1, keepdims=True)
        p_ref[b0:b0 + rows, :] = jnp.exp2(s - m).astype(BF16)
        o_ext = _dot(p_ref[b0:b0 + rows, :], v_s[...])
        o_ref[r0:r0 + rows, :] = (o_ext[:, 0:LANES] / o_ext[:, LANES:2 * LANES]).astype(o_ref.dtype)

    n_sub = len(sub_tiles)
    for idx in range(min(ATTN_LOOKAHEAD, n_sub)):
        scores(idx)
    groups = max(1, n_sub - ATTN_LOOKAHEAD - 1 - SIDE_SLACK)
    pending = {}
    for idx in range(n_sub):
        if idx in pending:
            ordering[0] = pending.pop(idx)
        if idx + ATTN_LOOKAHEAD < n_sub:
            scores(idx + ATTN_LOOKAHEAD)
        finish(idx)
        if idx < groups:
            zeros = [thunk() for thunk in
                     side_work[idx * len(side_work) // groups:(idx + 1) * len(side_work) // groups]]
            if zeros:
                pending[idx + 1 + SIDE_SLACK] = functools.reduce(lambda a, b: a + b, zeros)


def _fill_keys(dst_ref, col, make_rows, n_real, n_meta):
    dst_ref[0:n_real, col:col + LANES] = make_rows(n_meta, n_real)
    dst_ref[n_real:n_real + n_meta, col:col + LANES] = make_rows(0, n_meta)
    dst_ref[n_real + n_meta:n_real + LANES, col:col + LANES] = jnp.zeros((LANES - n_meta, LANES), BF16)


def _fill_ones(v_s):
    v_s[:, LANES:2 * LANES] = jnp.ones((v_s.shape[0], LANES), BF16)


def _gqa_kernel(q_ref, k_ref, v_ref, cos_ref, lo_ref, hi_ref, qg_ref, kg_ref, z_ref, dw_ref, cb_ref,
                o_ref, c_ref, k_s, v_s, s_ref, p_ref, zp_ref, *, n_real, n_meta):
    head = pl.program_id(1)
    shift = HEAD_DIM // 4

    @pl.when(lax.rem(head, GQA_GROUP) == 0)
    def _():
        def k_rows(start, size):
            x = _rms(k_ref[start:start + size, :].astype(F32), kg_ref[...])
            x = _rope(x, cos_ref[start:start + size, :], lo_ref[start:start + size, :],
                      hi_ref[start:start + size, :], shift)
            return x.astype(BF16)

        def v_rows(start, size):
            return v_ref[start:start + size, :]

        _fill_keys(k_s, 0, k_rows, n_real, n_meta)
        _fill_keys(v_s, 0, v_rows, n_real, n_meta)
        _fill_ones(v_s)

    q = _rms(q_ref[...].astype(F32), qg_ref[...])
    q = _rope(q, cos_ref[...], lo_ref[...], hi_ref[...], shift)
    q = (q * (LOG2E / math.sqrt(HEAD_DIM))).astype(BF16)
    seq = n_real + n_meta
    _conv_stage(z_ref, zp_ref, seq)
    _softmax_pv(q, k_s, v_s, s_ref, p_ref, o_ref, n_real, n_meta, GQA_SUB_TILES,
                side_work=_conv_chunk_thunks(zp_ref, dw_ref, cb_ref, c_ref, seq))


def _gqa_attention(pa, tabs, qn_g, kn_g, z, dw, cb, batch, seq):
    assert GQA_HEADS == C_CONV // LANES
    z3 = z.reshape(batch, seq, C_CONV)
    pa3 = pa.reshape(batch, seq, A_COLS)
    n_real = seq - N_META
    q_blk, k_blk, v_blk = A_Q // LANES, A_K // LANES, A_V // LANES
    tab = pl.BlockSpec((seq, LANES), lambda b, h: (0, 0))
    gspec = pl.BlockSpec((1, LANES), lambda b, h: (0, 0))
    out, conv = pl.pallas_call(
        functools.partial(_gqa_kernel, n_real=n_real, n_meta=N_META),
        grid=(batch, GQA_HEADS),
        in_specs=[
            pl.BlockSpec((None, seq, LANES), lambda b, h: (b, 0, q_blk + h)),
            pl.BlockSpec((None, seq, LANES), lambda b, h: (b, 0, k_blk + h // GQA_GROUP)),
            pl.BlockSpec((None, seq, LANES), lambda b, h: (b, 0, v_blk + h // GQA_GROUP)),
            tab, tab, tab, gspec, gspec,
            pl.BlockSpec((None, seq, LANES), lambda b, h: (b, 0, h)),
            pl.BlockSpec((CONV_K, LANES), lambda b, h: (0, h)),
            pl.BlockSpec((1, LANES), lambda b, h: (0, h)),
        ],
        out_specs=[pl.BlockSpec((None, seq, LANES), lambda b, h: (b, 0, h)),
                   pl.BlockSpec((None, seq, LANES), lambda b, h: (b, 0, h))],
        out_shape=[jax.ShapeDtypeStruct((batch, seq, GQA_HEADS * HEAD_DIM), BF16),
                   jax.ShapeDtypeStruct((batch, seq, C_CONV), BF16)],
        scratch_shapes=[pltpu.VMEM((n_real + LANES, HEAD_DIM), BF16),
                        pltpu.VMEM((n_real + LANES, 2 * LANES), BF16),
                        pltpu.VMEM((ATTN_RING * max(GQA_SUB_TILES), n_real + LANES), F32),
                        pltpu.VMEM((ATTN_RING * max(GQA_SUB_TILES), n_real + LANES), BF16),
                        pltpu.VMEM((CONV_PAD + seq + CONV_PAD, LANES), F32)],
        compiler_params=_cparams(2),
        name="gqa_attention",
    )(pa3, pa3, pa3, tabs[0], tabs[1], tabs[2],
      qn_g.reshape(1, HEAD_DIM), kn_g.reshape(1, HEAD_DIM), z3, dw, cb.reshape(1, C_CONV))
    return out.reshape(batch * seq, GQA_HEADS * HEAD_DIM), conv.reshape(batch * seq, C_CONV)


def _mla_kernel(q_ref, kv_ref, kpe_ref, cos_ref, lo_ref, hi_ref, o_ref, k_s, v_s, s_ref, p_ref, *, n_real, n_meta):
    shift = MLA_ROPE // 4

    def knope_rows(start, size):
        return kv_ref[start:start + size, 0:LANES]

    def kpe_rows(start, size):
        x = _rope(kpe_ref[start:start + size, :].astype(F32), cos_ref[start:start + size, :],
                  lo_ref[start:start + size, :], hi_ref[start:start + size, :], shift)
        return x.astype(BF16)

    def v_rows(start, size):
        return kv_ref[start:start + size, LANES:2 * LANES]

    @pl.when(pl.program_id(1) == 0)
    def _():
        _fill_keys(k_s, LANES, kpe_rows, n_real, n_meta)
        _fill_ones(v_s)

    _fill_keys(k_s, 0, knope_rows, n_real, n_meta)
    _fill_keys(v_s, 0, v_rows, n_real, n_meta)

    scale = LOG2E / math.sqrt(MLA_QK)
    q_nope = q_ref[:, 0:LANES].astype(F32) * scale
    q_pe = _rope(q_ref[:, LANES:2 * LANES].astype(F32), cos_ref[...], lo_ref[...], hi_ref[...], shift) * scale
    q = jnp.concatenate([q_nope, q_pe], axis=-1).astype(BF16)
    _softmax_pv(q, k_s, v_s, s_ref, p_ref, o_ref, n_real, n_meta, MLA_SUB_TILES)


def _mla_attention(q_mla, kv_mla, pa, tabs, batch, seq):
    n_real = seq - N_META
    width = 2 * LANES
    q3 = q_mla.reshape(batch, seq, MLA_HEADS * width)
    kv3 = kv_mla.reshape(batch, seq, MLA_HEADS * width)
    pa3 = pa.reshape(batch, seq, A_COLS)
    kpe_blk = A_KPE // LANES
    tab = pl.BlockSpec((seq, LANES), lambda b, h: (0, 0))
    out = pl.pallas_call(
        functools.partial(_mla_kernel, n_real=n_real, n_meta=N_META),
        grid=(batch, MLA_HEADS),
        in_specs=[
            pl.BlockSpec((None, seq, width), lambda b, h: (b, 0, h)),
            pl.BlockSpec((None, seq, width), lambda b, h: (b, 0, h)),
            pl.BlockSpec((None, seq, LANES), lambda b, h: (b, 0, kpe_blk)),
            tab, tab, tab,
        ],
        out_specs=pl.BlockSpec((None, seq, LANES), lambda b, h: (b, 0, h)),
        out_shape=jax.ShapeDtypeStruct((batch, seq, MLA_HEADS * LANES), BF16),
        scratch_shapes=[pltpu.VMEM((n_real + LANES, width), BF16),
                        pltpu.VMEM((n_real + LANES, 2 * LANES), BF16),
                        pltpu.VMEM((ATTN_RING * max(MLA_SUB_TILES), n_real + LANES), F32),
                        pltpu.VMEM((ATTN_RING * max(MLA_SUB_TILES), n_real + LANES), BF16)],
        compiler_params=_cparams(2),
        name="mla_attention",
    )(q3, kv3, pa3, tabs[0], tabs[1], tabs[2])
    return out.reshape(batch * seq, MLA_HEADS * LANES)


def _merge_kernel(c_ref, ob_ref, oc_ref, g0_ref, g1_ref, g2_ref, h_ref, lng_ref, lnb_ref,
                  wa_ref, wb_ref, wc_ref, wo_ref, o_ref):
    c = c_ref[...].astype(F32)
    mu = jnp.mean(c, axis=-1, keepdims=True)
    xc = c - mu
    y = xc * lax.rsqrt(jnp.mean(xc * xc, axis=-1, keepdims=True) + NORM_EPS)
    y = y * lng_ref[...] + lnb_ref[...]
    act = (y * jax.nn.sigmoid(y)).astype(BF16)
    merged = g0_ref[...].astype(F32) * _dot(act, wa_ref[...])
    merged = merged + g1_ref[...].astype(F32) * _dot(ob_ref[...], wb_ref[...])
    merged = merged + g2_ref[...].astype(F32) * _dot(oc_ref[...], wc_ref[...])
    o_ref[...] = h_ref[...] + _dot(merged.astype(BF16), wo_ref[...])


def _merge(c, o_b, o_c, gates, h, ln_g, ln_b, w_a, w_b, w_c, w_o, layer):
    t, d = h.shape
    bm = MERGE_ROW_TILE
    row = lambda width, col: pl.BlockSpec((bm, width), lambda i: (i, col))
    const = lambda shape: pl.BlockSpec(shape, lambda i: (0, 0), pipeline_mode=pl.Buffered(1))
    weight = lambda k: pl.BlockSpec((None, k, d), lambda i: (layer, 0, 0), pipeline_mode=pl.Buffered(1))
    return pl.pallas_call(
        _merge_kernel,
        grid=(t // bm,),
        in_specs=[row(C_CONV, 0), row(C_CONV, 0), row(C_CONV, 0),
                  row(d, 0), row(d, 1), row(d, 2), row(d, 0),
                  const((1, C_CONV)), const((1, C_CONV)),
                  weight(C_CONV), weight(C_CONV), weight(C_CONV), weight(d)],
        out_specs=row(d, 0),
        out_shape=jax.ShapeDtypeStruct((t, d), F32),
        compiler_params=_cparams(1),
        name="branch_merge",
    )(c, o_b, o_c, gates, gates, gates, h, ln_g.reshape(1, C_CONV), ln_b.reshape(1, C_CONV),
      w_a, w_b, w_c, w_o)


def _ffn_kernel(h_ref, g_ref, wg_ref, wu_ref, wd_ref, o_ref, v_ref):
    @pl.when(pl.program_id(1) == 0)
    def _():
        hv = h_ref[...]
        v_ref[...] = _rms(hv, g_ref[...]).astype(BF16)
        o_ref[...] = hv

    v = v_ref[...]
    gate = _dot(v, wg_ref[...])
    up = _dot(v, wu_ref[...])
    a = (gate * jax.nn.sigmoid(gate) * up).astype(BF16)
    o_ref[...] += _dot(a, wd_ref[...])


def _ffn(h, g, w_gate, w_up, w_down, layer):
    t, d = h.shape
    return pl.pallas_call(
        _ffn_kernel,
        grid=(t // ROW_TILE, D_FF // FF_TILE),
        in_specs=[
            pl.BlockSpec((ROW_TILE, d), lambda i, f: (i, 0)),
            pl.BlockSpec((1, d), lambda i, f: (0, 0)),
            pl.BlockSpec((None, d, FF_TILE), lambda i, f: (layer, 0, f)),
            pl.BlockSpec((None, d, FF_TILE), lambda i, f: (layer, 0, f)),
            pl.BlockSpec((None, FF_TILE, d), lambda i, f: (layer, f, 0)),
        ],
        out_specs=pl.BlockSpec((ROW_TILE, d), lambda i, f: (i, 0)),
        out_shape=jax.ShapeDtypeStruct((t, d), F32),
        scratch_shapes=[pltpu.VMEM((ROW_TILE, d), BF16)],
        compiler_params=_cparams(2),
        name="swiglu_ffn",
    )(h, g.reshape(1, d), w_gate, w_up, w_down)


def _final_norm_kernel(h_ref, g_ref, o_ref):
    o_ref[...] = _rms(h_ref[...], g_ref[...])


def _final_norm(h, g, batch, seq):
    d = h.shape[1]
    n_tok = seq - N_META
    return pl.pallas_call(
        _final_norm_kernel,
        grid=(batch, n_tok // FINAL_ROWS),
        in_specs=[pl.BlockSpec((pl.Element(FINAL_ROWS), pl.Element(d)),
                               lambda b, r: ((b * (seq // 8) + N_META // 8 + r * (FINAL_ROWS // 8)) * 8, 0)),
                  pl.BlockSpec((1, d), lambda b, r: (0, 0))],
        out_specs=pl.BlockSpec((None, FINAL_ROWS, d), lambda b, r: (b, r, 0)),
        out_shape=jax.ShapeDtypeStruct((batch, n_tok, d), F32),
        compiler_params=_cparams(2),
        name="final_rmsnorm",
    )(h, g.reshape(1, d))


def _rope_tables(n_tok, rot_dim):
    rows = n_tok // GRID_W
    row = jnp.concatenate([jnp.zeros((N_META,), F32), jnp.repeat(jnp.arange(rows, dtype=F32), GRID_W)])
    col = jnp.concatenate([jnp.zeros((N_META,), F32), jnp.tile(jnp.arange(GRID_W, dtype=F32), rows)])
    sub = rot_dim // 4
    inv = ROPE_THETA ** (-jnp.arange(sub, dtype=F32) / sub)
    ang_r = row[:, None] * inv[None, :]
    ang_c = col[:, None] * inv[None, :]
    cr, sr, cc, sc = jnp.cos(ang_r), jnp.sin(ang_r), jnp.cos(ang_c), jnp.sin(ang_c)
    zero = jnp.zeros_like(sr)
    l = row.shape[0]
    pad1 = jnp.ones((l, LANES - rot_dim), F32)
    pad0 = jnp.zeros((l, LANES - rot_dim), F32)
    cos = jnp.concatenate([cr, cr, cc, cc, pad1], axis=1)
    sin_lo = jnp.concatenate([-sr, zero, -sc, zero, pad0], axis=1)
    sin_hi = jnp.concatenate([zero, sr, zero, sc, pad0], axis=1)
    return cos, sin_lo, sin_hi


def _in_proj_segments():
    src = {}
    o = 0
    for name, width in (("a", C_CONV), ("gte", C_CONV), ("q", 1024), ("k", 256), ("v", 256),
                        ("cq", MLA_RANK), ("ckv", MLA_RANK), ("kpe", MLA_ROPE), ("gates", 3 * D_MODEL)):
        src[name] = o
        o += width
    half = IN_TILE // 2
    segs = []
    for t in range(GLU_TILES):
        segs.append((src["a"] + t * half, t * IN_TILE, half))
        segs.append((src["gte"] + t * half, t * IN_TILE + half, half))
    base = GLU_TILES * IN_TILE
    segs.append((src["gates"], base, 3 * D_MODEL))
    base += 3 * D_MODEL
    for name, off, width in (("q", A_Q, 1024), ("cq", A_CQ, MLA_RANK), ("ckv", A_CKV, MLA_RANK),
                             ("k", A_K, 256), ("v", A_V, 256), ("kpe", A_KPE, MLA_ROPE)):
        segs.append((src[name], base + off, width))
    return segs, base + A_KPE + MLA_ROPE, base + A_COLS


def _pack_w_in_kernel(x_ref, o_ref):
    segs, zero_from, total = _in_proj_segments()
    for s, d, w in segs:
        o_ref[:, d:d + w] = x_ref[:, s:s + w].astype(o_ref.dtype)
    o_ref[:, zero_from:total] = jnp.zeros((o_ref.shape[0], total - zero_from), o_ref.dtype)


def _pack_w_in(w_in):
    depth, d, n_in = w_in.shape
    total = _in_proj_segments()[2]
    return pl.pallas_call(
        _pack_w_in_kernel,
        grid=(depth, d // PACK_ROWS),
        in_specs=[pl.BlockSpec((None, PACK_ROWS, n_in), lambda a, r: (a, r, 0))],
        out_specs=pl.BlockSpec((None, PACK_ROWS, total), lambda a, r: (a, r, 0)),
        out_shape=jax.ShapeDtypeStruct((depth, d, total), BF16),
        compiler_params=_cparams(2),
        name="pack_w_in",
    )(w_in)


def _cast_kernel(x_ref, o_ref):
    o_ref[...] = x_ref[...].astype(o_ref.dtype)


def _to_bf16(w):
    depth, k, n = w.shape
    rows = min(k, PACK_ROWS)
    return pl.pallas_call(
        _cast_kernel,
        grid=(depth, k // rows),
        in_specs=[pl.BlockSpec((None, rows, n), lambda a, r: (a, r, 0))],
        out_specs=pl.BlockSpec((None, rows, n), lambda a, r: (a, r, 0)),
        out_shape=jax.ShapeDtypeStruct(w.shape, BF16),
        compiler_params=_cparams(2),
        name="cast_bf16",
    )(w)


def _pad_uq(w_uq):
    depth = w_uq.shape[0]
    wq = w_uq.reshape(depth, MLA_RANK, MLA_HEADS, MLA_QK)
    wq = jnp.concatenate([wq, jnp.zeros((depth, MLA_RANK, MLA_HEADS, 2 * LANES - MLA_QK), w_uq.dtype)], axis=3)
    return wq.reshape(depth, MLA_RANK, MLA_HEADS * 2 * LANES).astype(BF16)


def _in_proj_bias(gate_b):
    depth = gate_b.shape[0]
    n_glu = GLU_TILES * IN_TILE
    return jnp.concatenate([jnp.zeros((depth, 1, n_glu), F32), gate_b.astype(F32)[:, None, :],
                            jnp.zeros((depth, 1, A_COLS), F32)], axis=2)


def kernel(x, meta_tokens, mix_norm_g, w_in, conv_dw, conv_b, conv_ln_g, conv_ln_b, w_conv_out, gqa_q_norm_g, gqa_k_norm_g, w_gqa_out, mla_q_norm_g, w_mla_uq, mla_kv_norm_g, w_mla_ukv, w_mla_out, gate_b, w_out, ffn_norm_g, w_ffn_gate, w_ffn_up, w_ffn_down, final_norm_g):
    batch, n_tok, d = x.shape
    seq = N_META + n_tok
    depth = w_in.shape[0]
    assert d == D_MODEL and seq == Q_TILE and seq % CONV_ROWS == 0
    assert sum(GQA_SUB_TILES) == Q_TILE and sum(MLA_SUB_TILES) == Q_TILE
    assert all((batch * seq) % tile == 0 for tile in (ROW_TILE, IN_ROW_TILE, MERGE_ROW_TILE))

    meta = jnp.broadcast_to(meta_tokens.astype(x.dtype)[None], (batch, N_META, d))
    h = jnp.concatenate([meta, x], axis=1).reshape(batch * seq, d)

    tabs_gqa = _rope_tables(n_tok, HEAD_DIM)
    tabs_mla = _rope_tables(n_tok, MLA_ROPE)

    w_all = _pack_w_in(w_in)
    bias_all = _in_proj_bias(gate_b)
    w_uq = _pad_uq(w_mla_uq)
    w_ukv, w_a, w_b, w_c, w_o = (_to_bf16(w) for w in (w_mla_ukv, w_conv_out, w_gqa_out, w_mla_out, w_out))
    w_fg, w_fu, w_fd = (_to_bf16(w) for w in (w_ffn_gate, w_ffn_up, w_ffn_down))

    for i in range(depth):
        z, gates, pa = _in_proj(h, mix_norm_g[i], w_all, bias_all, i)

        o_b, c = _gqa_attention(pa, tabs_gqa, gqa_q_norm_g[i], gqa_k_norm_g[i],
                                z, conv_dw[i].reshape(CONV_K, C_CONV), conv_b[i], batch, seq)

        q_mla, kv_mla = _mla_up(pa, mla_q_norm_g[i], mla_kv_norm_g[i], w_uq, w_ukv, i)
        o_c = _mla_attention(q_mla, kv_mla, pa, tabs_mla, batch, seq)

        h = _merge(c, o_b, o_c, gates, h, conv_ln_g[i], conv_ln_b[i], w_a, w_b, w_c, w_o, i)

        h = _ffn(h, ffn_norm_g[i], w_fg, w_fu, w_fd, i)

    return _final_norm(h, final_norm_g, batch, seq)
```

```python
import functools
import math

import jax
import jax.numpy as jnp
from jax import lax
from jax.experimental import pallas as pl
from jax.experimental.pallas import tpu as pltpu

D_MODEL = 2048
N_META = 16
GRID_W = 64
ROPE_THETA = 10000.0
NORM_EPS = 1e-6
C_CONV = D_MODEL // 2
CONV_K = 31
GQA_HEADS = 8
GQA_KV_HEADS = 2
GQA_GROUP = GQA_HEADS // GQA_KV_HEADS
HEAD_DIM = 128
MLA_HEADS = 8
MLA_RANK = D_MODEL // 4
MLA_NOPE = 128
MLA_ROPE = 64
MLA_QK = MLA_NOPE + MLA_ROPE
D_FF = 5632

LANES = 128
MXU_DIM = 256
VMEM_LIMIT = 56 * 1024 * 1024

ROW_TILE = 768
IN_ROW_TILE = 768
MERGE_ROW_TILE = 256
FF_TILE = 512
FINAL_ROWS = 512
PACK_ROWS = 256
PACK_COLS = 512
Q_TILE = 2064
GQA_SUB_TILES = (176,) * 11 + (128,)
MLA_SUB_TILES = (416,) * 4 + (400,)
ATTN_RING = 4
ATTN_LOOKAHEAD = 2
SIDE_SLACK = 0
LOG2E = 1.4426950408889634
CONV_ROWS = 48
CONV_PAD = 16

A_Q, A_CQ, A_CKV, A_K, A_V, A_KPE, A_END = 0, 1024, 1536, 2048, 2304, 2560, 2688
IN_TILE = 1024
A_COLS = 3 * IN_TILE
GLU_TILES = 2 * C_CONV // IN_TILE
GATE_TILES = 3 * D_MODEL // IN_TILE
ATTN_TILES = A_COLS // IN_TILE

F32 = jnp.float32
BF16 = jnp.bfloat16
NEG_BIG = -1e30


def _cparams(n_axes):
    return pltpu.CompilerParams(
        dimension_semantics=("arbitrary",) * n_axes, vmem_limit_bytes=VMEM_LIMIT)


def _rms(x, g):
    return x * lax.rsqrt(jnp.mean(x * x, axis=-1, keepdims=True) + NORM_EPS) * g


def _rope(x, cos, sin_lo, sin_hi, shift):
    up = pltpu.roll(x, LANES - shift, 1)
    dn = pltpu.roll(x, shift, 1)
    return x * cos + up * sin_lo + dn * sin_hi


def _dot_nt(a, b):
    return lax.dot_general(a, b, (((1,), (1,)), ((), ())), preferred_element_type=F32)


def _dot(a, b):
    return jnp.dot(a, b, preferred_element_type=F32)


def _mla_up_kernel(cq_ref, ckv_ref, gq_ref, gkv_ref, wq_ref, wkv_ref, q_ref, kv_ref):
    q_ref[...] = _dot(_rms(cq_ref[...].astype(F32), gq_ref[...]).astype(BF16), wq_ref[...]).astype(q_ref.dtype)
    kv_ref[...] = _dot(_rms(ckv_ref[...].astype(F32), gkv_ref[...]).astype(BF16), wkv_ref[...]).astype(kv_ref.dtype)


def _mla_up(pa, g_q, g_kv, w_uq, w_ukv, layer):
    t = pa.shape[0]
    n = w_uq.shape[2]
    x_spec = lambda off: pl.BlockSpec((ROW_TILE, MLA_RANK), lambda i: (i, off // MLA_RANK))
    g_spec = pl.BlockSpec((1, MLA_RANK), lambda i: (0, 0))
    w_spec = pl.BlockSpec((None, MLA_RANK, n), lambda i: (layer, 0, 0))
    o_spec = pl.BlockSpec((ROW_TILE, n), lambda i: (i, 0))
    return pl.pallas_call(
        _mla_up_kernel,
        grid=(t // ROW_TILE,),
        in_specs=[x_spec(A_CQ), x_spec(A_CKV), g_spec, g_spec, w_spec, w_spec],
        out_specs=[o_spec, o_spec],
        out_shape=[jax.ShapeDtypeStruct((t, n), BF16), jax.ShapeDtypeStruct((t, n), BF16)],
        compiler_params=_cparams(1),
        name="mla_up_proj",
    )(pa, pa, g_q.reshape(1, MLA_RANK), g_kv.reshape(1, MLA_RANK), w_uq, w_ukv)


def _in_proj_kernel(x_ref, g_ref, w_ref, b_ref, z_ref, gates_ref, pa_ref, xn_ref):
    j = pl.program_id(1)

    @pl.when(j == 0)
    def _():
        xn_ref[...] = _rms(x_ref[...], g_ref[...]).astype(BF16)

    @pl.when(j < GLU_TILES)
    def _():
        acc = _dot(xn_ref[...], w_ref[...])
        half = IN_TILE // 2
        z_ref[...] = (acc[:, :half] * jax.nn.sigmoid(acc[:, half:])).astype(z_ref.dtype)

    @pl.when((j >= GLU_TILES) & (j < GLU_TILES + GATE_TILES))
    def _():
        acc = _dot(xn_ref[...], w_ref[...])
        gates_ref[...] = jax.nn.sigmoid(acc + b_ref[...]).astype(gates_ref.dtype)

    @pl.when(j >= GLU_TILES + GATE_TILES)
    def _():
        pa_ref[...] = _dot(xn_ref[...], w_ref[...]).astype(pa_ref.dtype)


def _in_proj(h, g, w_all, bias_all, layer):
    t, d = h.shape
    n_tiles = GLU_TILES + GATE_TILES + ATTN_TILES
    clamp = lambda j, lo, n: jnp.clip(j - lo, 0, n - 1)
    bm = IN_ROW_TILE
    return pl.pallas_call(
        _in_proj_kernel,
        grid=(t // bm, n_tiles),
        in_specs=[
            pl.BlockSpec((bm, d), lambda i, j: (i, 0)),
            pl.BlockSpec((1, d), lambda i, j: (0, 0)),
            pl.BlockSpec((None, d, IN_TILE), lambda i, j: (layer, 0, j)),
            pl.BlockSpec((None, 1, IN_TILE), lambda i, j: (layer, 0, j)),
        ],
        out_specs=[
            pl.BlockSpec((bm, IN_TILE // 2), lambda i, j: (i, clamp(j, 0, GLU_TILES))),
            pl.BlockSpec((bm, IN_TILE), lambda i, j: (i, clamp(j, GLU_TILES, GATE_TILES))),
            pl.BlockSpec((bm, IN_TILE), lambda i, j: (i, clamp(j, GLU_TILES + GATE_TILES, ATTN_TILES))),
        ],
        out_shape=[jax.ShapeDtypeStruct((t, C_CONV), BF16),
                   jax.ShapeDtypeStruct((t, 3 * D_MODEL), BF16),
                   jax.ShapeDtypeStruct((t, A_COLS), BF16)],
        scratch_shapes=[pltpu.VMEM((bm, d), BF16)],
        compiler_params=_cparams(2),
        name="in_proj",
    )(h, g.reshape(1, d), w_all, bias_all)


def _conv_stage(z_ref, zp_ref, seq):
    tail = zp_ref.shape[0] - CONV_PAD - seq
    zp_ref[0:CONV_PAD, :] = jnp.zeros((CONV_PAD, LANES), F32)
    zp_ref[CONV_PAD + seq:, :] = jnp.zeros((tail, LANES), F32)
    zp_ref[CONV_PAD:CONV_PAD + seq, :] = z_ref[...].astype(F32)


def _conv_chunk_thunks(zp_ref, dw_ref, cb_ref, o_ref, seq):
    def make(base):
        def thunk():
            acc = jnp.zeros((CONV_ROWS, LANES), F32) + cb_ref[...]
            for j in range(CONV_K):
                off = CONV_PAD - CONV_K // 2 + j
                acc = acc + zp_ref[base + off:base + off + CONV_ROWS, :] * dw_ref[j:j + 1, :]
            o_ref[base:base + CONV_ROWS, :] = acc.astype(o_ref.dtype)
            bits = pltpu.bitcast(acc[0:8, :], jnp.uint32)
            zero = lax.shift_right_logical(lax.shift_right_logical(bits, jnp.uint32(16)), jnp.uint32(16))
            return pltpu.bitcast(zero, F32)[0:1, :]
        return thunk
    return [make(c * CONV_ROWS) for c in range(seq // CONV_ROWS)]


def _softmax_pv(q, k_s, v_s, s_ref, p_ref, o_ref, n_real, n_meta, sub_tiles, side_work=()):
    lane = lax.broadcasted_iota(jnp.int32, (1, LANES), 1)
    tail_bias = jnp.where(lane < n_meta, 0.0, NEG_BIG).astype(F32)
    slot_rows = max(sub_tiles)
    starts = [sum(sub_tiles[:i]) for i in range(len(sub_tiles))]

    ordering = [jnp.zeros((1, LANES), F32)]

    def scores(idx):
        r0, rows, b0 = starts[idx], sub_tiles[idx], (idx % ATTN_RING) * slot_rows
        qs = q[r0:r0 + rows, :]
        s_ref[b0:b0 + rows, 0:n_real] = _dot_nt(qs, k_s[0:n_real, :])
        s_ref[b0:b0 + rows, n_real:n_real + LANES] = (
            _dot_nt(qs, k_s[n_real:n_real + LANES, :]) + (tail_bias + ordering[0]))

    def finish(idx):
        r0, rows, b0 = starts[idx], sub_tiles[idx], (idx % ATTN_RING) * slot_rows
        s = s_ref[b0:b0 + rows, :]
        m = jnp.max(s, axis=-1, keepdims=True)
        p_ref[b0:b0 + rows, :] = jnp.exp2(s - m).astype(BF16)
        o_ext = _dot(p_ref[b0:b0 + rows, :], v_s[...])
        o_ref[r0:r0 + rows, :] = (o_ext[:, 0:LANES] / o_ext[:, LANES:2 * LANES]).astype(o_ref.dtype)

    n_sub = len(sub_tiles)
    for idx in range(min(ATTN_LOOKAHEAD, n_sub)):
        scores(idx)
    groups = max(1, n_sub - ATTN_LOOKAHEAD - 1 - SIDE_SLACK)
    pending = {}
    for idx in range(n_sub):
        if idx in pending:
            ordering[0] = pending.pop(idx)
        if idx + ATTN_LOOKAHEAD < n_sub:
            scores(idx + ATTN_LOOKAHEAD)
        finish(idx)
        if idx < groups:
            zeros = [thunk() for thunk in
                     side_work[idx * len(side_work) // groups:(idx + 1) * len(side_work) // groups]]
            if zeros:
                pending[idx + 1 + SIDE_SLACK] = functools.reduce(lambda a, b: a + b, zeros)


def _fill_keys(dst_ref, col, make_rows, n_real, n_meta):
    dst_ref[0:n_real, col:col + LANES] = make_rows(n_meta, n_real)
    dst_ref[n_real:n_real + n_meta, col:col + LANES] = make_rows(0, n_meta)
    dst_ref[n_real + n_meta:n_real + LANES, col:col + LANES] = jnp.zeros((LANES - n_meta, LANES), BF16)


def _fill_ones(v_s):
    v_s[:, LANES:2 * LANES] = jnp.ones((v_s.shape[0], LANES), BF16)


def _gqa_kernel(q_ref, k_ref, v_ref, cos_ref, lo_ref, hi_ref, qg_ref, kg_ref, z_ref, dw_ref, cb_ref,
                o_ref, c_ref, k_s, v_s, s_ref, p_ref, zp_ref, *, n_real, n_meta):
    head = pl.program_id(1)
    shift = HEAD_DIM // 4

    @pl.when(lax.rem(head, GQA_GROUP) == 0)
    def _():
        def k_rows(start, size):
            x = _rms(k_ref[start:start + size, :].astype(F32), kg_ref[...])
            x = _rope(x, cos_ref[start:start + size, :], lo_ref[start:start + size, :],
                      hi_ref[start:start + size, :], shift)
            return x.astype(BF16)

        def v_rows(start, size):
            return v_ref[start:start + size, :]

        _fill_keys(k_s, 0, k_rows, n_real, n_meta)
        _fill_keys(v_s, 0, v_rows, n_real, n_meta)
        _fill_ones(v_s)

    q = _rms(q_ref[...].astype(F32), qg_ref[...])
    q = _rope(q, cos_ref[...], lo_ref[...], hi_ref[...], shift)
    q = (q * (LOG2E / math.sqrt(HEAD_DIM))).astype(BF16)
    seq = n_real + n_meta
    _conv_stage(z_ref, zp_ref, seq)
    _softmax_pv(q, k_s, v_s, s_ref, p_ref, o_ref, n_real, n_meta, GQA_SUB_TILES,
                side_work=_conv_chunk_thunks(zp_ref, dw_ref, cb_ref, c_ref, seq))


def _gqa_attention(pa, tabs, qn_g, kn_g, z, dw, cb, batch, seq):
    assert GQA_HEADS == C_CONV // LANES
    z3 = z.reshape(batch, seq, C_CONV)
    pa3 = pa.reshape(batch, seq, A_COLS)
    n_real = seq - N_META
    q_blk, k_blk, v_blk = A_Q // LANES, A_K // LANES, A_V // LANES
    tab = pl.BlockSpec((seq, LANES), lambda b, h: (0, 0))
    gspec = pl.BlockSpec((1, LANES), lambda b, h: (0, 0))
    out, conv = pl.pallas_call(
        functools.partial(_gqa_kernel, n_real=n_real, n_meta=N_META),
        grid=(batch, GQA_HEADS),
        in_specs=[
            pl.BlockSpec((None, seq, LANES), lambda b, h: (b, 0, q_blk + h)),
            pl.BlockSpec((None, seq, LANES), lambda b, h: (b, 0, k_blk + h // GQA_GROUP)),
            pl.BlockSpec((None, seq, LANES), lambda b, h: (b, 0, v_blk + h // GQA_GROUP)),
            tab, tab, tab, gspec, gspec,
            pl.BlockSpec((None, seq, LANES), lambda b, h: (b, 0, h)),
            pl.BlockSpec((CONV_K, LANES), lambda b, h: (0, h)),
            pl.BlockSpec((1, LANES), lambda b, h: (0, h)),
        ],
        out_specs=[pl.BlockSpec((None, seq, LANES), lambda b, h: (b, 0, h)),
                   pl.BlockSpec((None, seq, LANES), lambda b, h: (b, 0, h))],
        out_shape=[jax.ShapeDtypeStruct((batch, seq, GQA_HEADS * HEAD_DIM), BF16),
                   jax.ShapeDtypeStruct((batch, seq, C_CONV), BF16)],
        scratch_shapes=[pltpu.VMEM((n_real + LANES, HEAD_DIM), BF16),
                        pltpu.VMEM((n_real + LANES, 2 * LANES), BF16),
                        pltpu.VMEM((ATTN_RING * max(GQA_SUB_TILES), n_real + LANES), F32),
                        pltpu.VMEM((ATTN_RING * max(GQA_SUB_TILES), n_real + LANES), BF16),
                        pltpu.VMEM((CONV_PAD + seq + CONV_PAD, LANES), F32)],
        compiler_params=_cparams(2),
        name="gqa_attention",
    )(pa3, pa3, pa3, tabs[0], tabs[1], tabs[2],
      qn_g.reshape(1, HEAD_DIM), kn_g.reshape(1, HEAD_DIM), z3, dw, cb.reshape(1, C_CONV))
    return out.reshape(batch * seq, GQA_HEADS * HEAD_DIM), conv.reshape(batch * seq, C_CONV)


def _mla_kernel(q_ref, kv_ref, kpe_ref, cos_ref, lo_ref, hi_ref, o_ref, k_s, v_s, s_ref, p_ref, *, n_real, n_meta):
    shift = MLA_ROPE // 4

    def knope_rows(start, size):
        return kv_ref[start:start + size, 0:LANES]

    def kpe_rows(start, size):
        x = _rope(kpe_ref[start:start + size, :].astype(F32), cos_ref[start:start + size, :],
                  lo_ref[start:start + size, :], hi_ref[start:start + size, :], shift)
        return x.astype(BF16)

    def v_rows(start, size):
        return kv_ref[start:start + size, LANES:2 * LANES]

    @pl.when(pl.program_id(1) == 0)
    def _():
        _fill_keys(k_s, LANES, kpe_rows, n_real, n_meta)
        _fill_ones(v_s)

    _fill_keys(k_s, 0, knope_rows, n_real, n_meta)
    _fill_keys(v_s, 0, v_rows, n_real, n_meta)

    scale = LOG2E / math.sqrt(MLA_QK)
    q_nope = q_ref[:, 0:LANES].astype(F32) * scale
    q_pe = _rope(q_ref[:, LANES:2 * LANES].astype(F32), cos_ref[...], lo_ref[...], hi_ref[...], shift) * scale
    q = jnp.concatenate([q_nope, q_pe], axis=-1).astype(BF16)
    _softmax_pv(q, k_s, v_s, s_ref, p_ref, o_ref, n_real, n_meta, MLA_SUB_TILES)


def _mla_attention(q_mla, kv_mla, pa, tabs, batch, seq):
    n_real = seq - N_META
    width = 2 * LANES
    q3 = q_mla.reshape(batch, seq, MLA_HEADS * width)
    kv3 = kv_mla.reshape(batch, seq, MLA_HEADS * width)
    pa3 = pa.reshape(batch, seq, A_COLS)
    kpe_blk = A_KPE // LANES
    tab = pl.BlockSpec((seq, LANES), lambda b, h: (0, 0))
    out = pl.pallas_call(
        functools.partial(_mla_kernel, n_real=n_real, n_meta=N_META),
        grid=(batch, MLA_HEADS),
        in_specs=[
            pl.BlockSpec((None, seq, width), lambda b, h: (b, 0, h)),
            pl.BlockSpec((None, seq, width), lambda b, h: (b, 0, h)),
            pl.BlockSpec((None, seq, LANES), lambda b, h: (b, 0, kpe_blk)),
            tab, tab, tab,
        ],
        out_specs=pl.BlockSpec((None, seq, LANES), lambda b, h: (b, 0, h)),
        out_shape=jax.ShapeDtypeStruct((batch, seq, MLA_HEADS * LANES), BF16),
        scratch_shapes=[pltpu.VMEM((n_real + LANES, width), BF16),
                        pltpu.VMEM((n_real + LANES, 2 * LANES), BF16),
                        pltpu.VMEM((ATTN_RING * max(MLA_SUB_TILES), n_real + LANES), F32),
                        pltpu.VMEM((ATTN_RING * max(MLA_SUB_TILES), n_real + LANES), BF16)],
        compiler_params=_cparams(2),
        name="mla_attention",
    )(q3, kv3, pa3, tabs[0], tabs[1], tabs[2])
    return out.reshape(batch * seq, MLA_HEADS * LANES)


def _merge_kernel(c_ref, ob_ref, oc_ref, g0_ref, g1_ref, g2_ref, h_ref, lng_ref, lnb_ref,
                  wa_ref, wb_ref, wc_ref, wo_ref, o_ref):
    c = c_ref[...].astype(F32)
    mu = jnp.mean(c, axis=-1, keepdims=True)
    xc = c - mu
    y = xc * lax.rsqrt(jnp.mean(xc * xc, axis=-1, keepdims=True) + NORM_EPS)
    y = y * lng_ref[...] + lnb_ref[...]
    act = (y * jax.nn.sigmoid(y)).astype(BF16)
    merged = g0_ref[...].astype(F32) * _dot(act, wa_ref[...])
    merged = merged + g1_ref[...].astype(F32) * _dot(ob_ref[...], wb_ref[...])
    merged = merged + g2_ref[...].astype(F32) * _dot(oc_ref[...], wc_ref[...])
    o_ref[...] = h_ref[...] + _dot(merged.astype(BF16), wo_ref[...])


def _merge(c, o_b, o_c, gates, h, ln_g, ln_b, w_a, w_b, w_c, w_o, layer):
    t, d = h.shape
    bm = MERGE_ROW_TILE
    row = lambda width, col: pl.BlockSpec((bm, width), lambda i: (i, col))
    const = lambda shape: pl.BlockSpec(shape, lambda i: (0, 0), pipeline_mode=pl.Buffered(1))
    weight = lambda k: pl.BlockSpec((None, k, d), lambda i: (layer, 0, 0), pipeline_mode=pl.Buffered(1))
    return pl.pallas_call(
        _merge_kernel,
        grid=(t // bm,),
        in_specs=[row(C_CONV, 0), row(C_CONV, 0), row(C_CONV, 0),
                  row(d, 0), row(d, 1), row(d, 2), row(d, 0),
                  const((1, C_CONV)), const((1, C_CONV)),
                  weight(C_CONV), weight(C_CONV), weight(C_CONV), weight(d)],
        out_specs=row(d, 0),
        out_shape=jax.ShapeDtypeStruct((t, d), F32),
        compiler_params=_cparams(1),
        name="branch_merge",
    )(c, o_b, o_c, gates, gates, gates, h, ln_g.reshape(1, C_CONV), ln_b.reshape(1, C_CONV),
      w_a, w_b, w_c, w_o)


def _ffn_kernel(h_ref, g_ref, wg_ref, wu_ref, wd_ref, o_ref, v_ref):
    @pl.when(pl.program_id(1) == 0)
    def _():
        hv = h_ref[...]
        v_ref[...] = _rms(hv, g_ref[...]).astype(BF16)
        o_ref[...] = hv

    v = v_ref[...]
    gate = _dot(v, wg_ref[...])
    up = _dot(v, wu_ref[...])
    a = (gate * jax.nn.sigmoid(gate) * up).astype(BF16)
    o_ref[...] += _dot(a, wd_ref[...])


def _ffn(h, g, w_gate, w_up, w_down, layer):
    t, d = h.shape
    return pl.pallas_call(
        _ffn_kernel,
        grid=(t // ROW_TILE, D_FF // FF_TILE),
        in_specs=[
            pl.BlockSpec((ROW_TILE, d), lambda i, f: (i, 0)),
            pl.BlockSpec((1, d), lambda i, f: (0, 0)),
            pl.BlockSpec((None, d, FF_TILE), lambda i, f: (layer, 0, f)),
            pl.BlockSpec((None, d, FF_TILE), lambda i, f: (layer, 0, f)),
            pl.BlockSpec((None, FF_TILE, d), lambda i, f: (layer, f, 0)),
        ],
        out_specs=pl.BlockSpec((ROW_TILE, d), lambda i, f: (i, 0)),
        out_shape=jax.ShapeDtypeStruct((t, d), F32),
        scratch_shapes=[pltpu.VMEM((ROW_TILE, d), BF16)],
        compiler_params=_cparams(2),
        name="swiglu_ffn",
    )(h, g.reshape(1, d), w_gate, w_up, w_down)


def _final_norm_kernel(h_ref, g_ref, o_ref):
    o_ref[...] = _rms(h_ref[...], g_ref[...])


def _final_norm(h, g, batch, seq):
    d = h.shape[1]
    n_tok = seq - N_META
    return pl.pallas_call(
        _final_norm_kernel,
        grid=(batch, n_tok // FINAL_ROWS),
        in_specs=[pl.BlockSpec((pl.Element(FINAL_ROWS), pl.Element(d)),
                               lambda b, r: ((b * (seq // 8) + N_META // 8 + r * (FINAL_ROWS // 8)) * 8, 0)),
                  pl.BlockSpec((1, d), lambda b, r: (0, 0))],
        out_specs=pl.BlockSpec((None, FINAL_ROWS, d), lambda b, r: (b, r, 0)),
        out_shape=jax.ShapeDtypeStruct((batch, n_tok, d), F32),
        compiler_params=_cparams(2),
        name="final_rmsnorm",
    )(h, g.reshape(1, d))


def _rope_tables(n_tok, rot_dim):
    rows = n_tok // GRID_W
    row = jnp.concatenate([jnp.zeros((N_META,), F32), jnp.repeat(jnp.arange(rows, dtype=F32), GRID_W)])
    col = jnp.concatenate([jnp.zeros((N_META,), F32), jnp.tile(jnp.arange(GRID_W, dtype=F32), rows)])
    sub = rot_dim // 4
    inv = ROPE_THETA ** (-jnp.arange(sub, dtype=F32) / sub)
    ang_r = row[:, None] * inv[None, :]
    ang_c = col[:, None] * inv[None, :]
    cr, sr, cc, sc = jnp.cos(ang_r), jnp.sin(ang_r), jnp.cos(ang_c), jnp.sin(ang_c)
    zero = jnp.zeros_like(sr)
    l = row.shape[0]
    pad1 = jnp.ones((l, LANES - rot_dim), F32)
    pad0 = jnp.zeros((l, LANES - rot_dim), F32)
    cos = jnp.concatenate([cr, cr, cc, cc, pad1], axis=1)
    sin_lo = jnp.concatenate([-sr, zero, -sc, zero, pad0], axis=1)
    sin_hi = jnp.concatenate([zero, sr, zero, sc, pad0], axis=1)
    return cos, sin_lo, sin_hi


def _in_proj_segments():
    src = {}
    o = 0
    for name, width in (("a", C_CONV), ("gte", C_CONV), ("q", 1024), ("k", 256), ("v", 256),
                        ("cq", MLA_RANK), ("ckv", MLA_RANK), ("kpe", MLA_ROPE), ("gates", 3 * D_MODEL)):
        src[name] = o
        o += width
    half = IN_TILE // 2
    segs = []
    for t in range(GLU_TILES):
        segs.append((src["a"] + t * half, t * IN_TILE, half))
        segs.append((src["gte"] + t * half, t * IN_TILE + half, half))
    base = GLU_TILES * IN_TILE
    segs.append((src["gates"], base, 3 * D_MODEL))
    base += 3 * D_MODEL
    for name, off, width in (("q", A_Q, 1024), ("cq", A_CQ, MLA_RANK), ("ckv", A_CKV, MLA_RANK),
                             ("k", A_K, 256), ("v", A_V, 256), ("kpe", A_KPE, MLA_ROPE)):
        segs.append((src[name], base + off, width))
    return segs, base + A_KPE + MLA_ROPE, base + A_COLS


def _pack_sources():
    raw, _, total = _in_proj_segments()
    segs = []
    for s, d, w in sorted(raw, key=lambda seg: seg[1]):
        if segs and segs[-1][0] + segs[-1][2] == s and segs[-1][1] + segs[-1][2] == d:
            segs[-1] = (segs[-1][0], segs[-1][1], segs[-1][2] + w)
        else:
            segs.append((s, d, w))
    table = []
    for dst in range(0, total, PACK_COLS):
        hit = [(s + dst - d, min(w - (dst - d), PACK_COLS)) for s, d, w in segs if d <= dst < d + w]
        table.append(hit[0] if hit else (0, 0))
    return table


def _pack_w_in_kernel(x_ref, o_ref):
    blk = pl.program_id(1)
    valid = functools.reduce(lambda acc, kv: jnp.where(blk == kv[0], kv[1][1], acc),
                             enumerate(_pack_sources()), 0)
    row = lax.broadcasted_iota(jnp.int32, (PACK_COLS, 1), 0)
    x = jnp.where(row < valid, x_ref[0], 0.0)
    o_ref[...] = x.T.astype(o_ref.dtype)


def _pack_w_in(w_in):
    depth, d, n_in = w_in.shape
    table = _pack_sources()
    assert all(s % 8 == 0 and s + PACK_COLS <= n_in for s, _ in table)

    def src_row(blk):
        off8 = functools.reduce(lambda acc, kv: jnp.where(blk == kv[0], kv[1][0] // 8, acc),
                                enumerate(table), 0)
        return off8 * 8

    w_t = jnp.swapaxes(w_in, 1, 2)
    return pl.pallas_call(
        _pack_w_in_kernel,
        grid=(depth, len(table)),
        in_specs=[pl.BlockSpec((pl.Element(1), pl.Element(PACK_COLS), pl.Element(d)),
                               lambda a, blk: (a, src_row(blk), 0))],
        out_specs=pl.BlockSpec((None, d, PACK_COLS), lambda a, blk: (a, 0, blk)),
        out_shape=jax.ShapeDtypeStruct((depth, d, len(table) * PACK_COLS), BF16),
        compiler_params=_cparams(2),
        name="pack_w_in",
    )(w_t)


def _cast_kernel(x_ref, o_ref):
    o_ref[...] = x_ref[...].astype(o_ref.dtype)


def _to_bf16(w):
    depth, k, n = w.shape
    rows = min(k, PACK_ROWS)
    return pl.pallas_call(
        _cast_kernel,
        grid=(depth, k // rows),
        in_specs=[pl.BlockSpec((None, rows, n), lambda a, r: (a, r, 0))],
        out_specs=pl.BlockSpec((None, rows, n), lambda a, r: (a, r, 0)),
        out_shape=jax.ShapeDtypeStruct(w.shape, BF16),
        compiler_params=_cparams(2),
        name="cast_bf16",
    )(w)


def _pad_uq(w_uq):
    depth = w_uq.shape[0]
    wq = w_uq.reshape(depth, MLA_RANK, MLA_HEADS, MLA_QK)
    wq = jnp.concatenate([wq, jnp.zeros((depth, MLA_RANK, MLA_HEADS, 2 * LANES - MLA_QK), w_uq.dtype)], axis=3)
    return wq.reshape(depth, MLA_RANK, MLA_HEADS * 2 * LANES).astype(BF16)


def _in_proj_bias(gate_b):
    depth = gate_b.shape[0]
    n_glu = GLU_TILES * IN_TILE
    return jnp.concatenate([jnp.zeros((depth, 1, n_glu), F32), gate_b.astype(F32)[:, None, :],
                            jnp.zeros((depth, 1, A_COLS), F32)], axis=2)


def kernel(x, meta_tokens, mix_norm_g, w_in, conv_dw, conv_b, conv_ln_g, conv_ln_b, w_conv_out, gqa_q_norm_g, gqa_k_norm_g, w_gqa_out, mla_q_norm_g, w_mla_uq, mla_kv_norm_g, w_mla_ukv, w_mla_out, gate_b, w_out, ffn_norm_g, w_ffn_gate, w_ffn_up, w_ffn_down, final_norm_g):
    batch, n_tok, d = x.shape
    seq = N_META + n_tok
    depth = w_in.shape[0]
    assert d == D_MODEL and seq == Q_TILE and seq % CONV_ROWS == 0
    assert sum(GQA_SUB_TILES) == Q_TILE and sum(MLA_SUB_TILES) == Q_TILE
    assert all((batch * seq) % tile == 0 for tile in (ROW_TILE, IN_ROW_TILE, MERGE_ROW_TILE))

    meta = jnp.broadcast_to(meta_tokens.astype(x.dtype)[None], (batch, N_META, d))
    h = jnp.concatenate([meta, x], axis=1).reshape(batch * seq, d)

    tabs_gqa = _rope_tables(n_tok, HEAD_DIM)
    tabs_mla = _rope_tables(n_tok, MLA_ROPE)

    w_all = _pack_w_in(w_in)
    bias_all = _in_proj_bias(gate_b)
    w_uq = _pad_uq(w_mla_uq)
    w_ukv, w_a, w_b, w_c, w_o = (_to_bf16(w) for w in (w_mla_ukv, w_conv_out, w_gqa_out, w_mla_out, w_out))
    w_fg, w_fu, w_fd = (_to_bf16(w) for w in (w_ffn_gate, w_ffn_up, w_ffn_down))

    for i in range(depth):
        z, gates, pa = _in_proj(h, mix_norm_g[i], w_all, bias_all, i)

        o_b, c = _gqa_attention(pa, tabs_gqa, gqa_q_norm_g[i], gqa_k_norm_g[i],
                                z, conv_dw[i].reshape(CONV_K, C_CONV), conv_b[i], batch, seq)

        q_mla, kv_mla = _mla_up(pa, mla_q_norm_g[i], mla_kv_norm_g[i], w_uq, w_ukv, i)
        o_c = _mla_attention(q_mla, kv_mla, pa, tabs_mla, batch, seq)

        h = _merge(c, o_b, o_c, gates, h, conv_ln_g[i], conv_ln_b[i], w_a, w_b, w_c, w_o, i)

        h = _ffn(h, ffn_norm_g[i], w_fg, w_fu, w_fd, i)

    return _final_norm(h, final_norm_g, batch, seq)
```

```python
import functools
import math

import jax
import jax.numpy as jnp
from jax import lax
from jax.experimental import pallas as pl
from jax.experimental.pallas import tpu as pltpu

D_MODEL = 2048
N_META = 16
GRID_W = 64
ROPE_THETA = 10000.0
NORM_EPS = 1e-6
C_CONV = D_MODEL // 2
CONV_K = 31
GQA_HEADS = 8
GQA_KV_HEADS = 2
GQA_GROUP = GQA_HEADS // GQA_KV_HEADS
HEAD_DIM = 128
MLA_HEADS = 8
MLA_RANK = D_MODEL // 4
MLA_NOPE = 128
MLA_ROPE = 64
MLA_QK = MLA_NOPE + MLA_ROPE
D_FF = 5632

LANES = 128
MXU_DIM = 256
VMEM_LIMIT = 56 * 1024 * 1024

ROW_TILE = 768
IN_ROW_TILE = 768
MERGE_ROW_TILE = 256
FF_TILE = 512
FINAL_ROWS = 512
PACK_ROWS = 256
PACK_COLS = 512
Q_TILE = 2064
GQA_SUB_TILES = (176,) * 11 + (128,)
MLA_SUB_TILES = (416,) * 4 + (400,)
ATTN_RING = 4
ATTN_LOOKAHEAD = 2
SIDE_SLACK = 0
LOG2E = 1.4426950408889634
CONV_ROWS = 48
CONV_PAD = 16

A_Q, A_CQ, A_CKV, A_K, A_V, A_KPE, A_END = 0, 1024, 1536, 2048, 2304, 2560, 2688
IN_TILE = 1024
A_COLS = 3 * IN_TILE
GLU_TILES = 2 * C_CONV // IN_TILE
GATE_TILES = 3 * D_MODEL // IN_TILE
ATTN_TILES = A_COLS // IN_TILE

F32 = jnp.float32
BF16 = jnp.bfloat16
NEG_BIG = -1e30


def _cparams(n_axes):
    return pltpu.CompilerParams(
        dimension_semantics=("arbitrary",) * n_axes, vmem_limit_bytes=VMEM_LIMIT)


def _rms(x, g):
    return x * lax.rsqrt(jnp.mean(x * x, axis=-1, keepdims=True) + NORM_EPS) * g


def _rope(x, cos, sin_lo, sin_hi, shift):
    up = pltpu.roll(x, LANES - shift, 1)
    dn = pltpu.roll(x, shift, 1)
    return x * cos + up * sin_lo + dn * sin_hi


def _dot_nt(a, b):
    return lax.dot_general(a, b, (((1,), (1,)), ((), ())), preferred_element_type=F32)


def _dot(a, b):
    return jnp.dot(a, b, preferred_element_type=F32)


def _mla_up_kernel(cq_ref, ckv_ref, gq_ref, gkv_ref, wq_ref, wkv_ref, q_ref, kv_ref):
    q_ref[...] = _dot(_rms(cq_ref[...].astype(F32), gq_ref[...]).astype(BF16), wq_ref[...]).astype(q_ref.dtype)
    kv_ref[...] = _dot(_rms(ckv_ref[...].astype(F32), gkv_ref[...]).astype(BF16), wkv_ref[...]).astype(kv_ref.dtype)


def _mla_up(pa, g_q, g_kv, w_uq, w_ukv, layer):
    t = pa.shape[0]
    n = w_uq.shape[2]
    x_spec = lambda off: pl.BlockSpec((ROW_TILE, MLA_RANK), lambda i: (i, off // MLA_RANK))
    g_spec = pl.BlockSpec((1, MLA_RANK), lambda i: (0, 0))
    w_spec = pl.BlockSpec((None, MLA_RANK, n), lambda i: (layer, 0, 0))
    o_spec = pl.BlockSpec((ROW_TILE, n), lambda i: (i, 0))
    return pl.pallas_call(
        _mla_up_kernel,
        grid=(t // ROW_TILE,),
        in_specs=[x_spec(A_CQ), x_spec(A_CKV), g_spec, g_spec, w_spec, w_spec],
        out_specs=[o_spec, o_spec],
        out_shape=[jax.ShapeDtypeStruct((t, n), BF16), jax.ShapeDtypeStruct((t, n), BF16)],
        compiler_params=_cparams(1),
        name="mla_up_proj",
    )(pa, pa, g_q.reshape(1, MLA_RANK), g_kv.reshape(1, MLA_RANK), w_uq, w_ukv)


def _in_proj_kernel(x_ref, g_ref, w_ref, b_ref, z_ref, gates_ref, pa_ref, xn_ref):
    j = pl.program_id(1)

    def glu(xn):
        acc = _dot(xn, w_ref[...])
        half = IN_TILE // 2
        z_ref[...] = (acc[:, :half] * jax.nn.sigmoid(acc[:, half:])).astype(z_ref.dtype)

    @pl.when(j == 0)
    def _():
        xn = _rms(x_ref[...], g_ref[...]).astype(BF16)
        xn_ref[...] = xn
        glu(xn)

    @pl.when((j > 0) & (j < GLU_TILES))
    def _():
        glu(xn_ref[...])

    @pl.when((j >= GLU_TILES) & (j < GLU_TILES + GATE_TILES))
    def _():
        acc = _dot(xn_ref[...], w_ref[...])
        gates_ref[...] = jax.nn.sigmoid(acc + b_ref[...]).astype(gates_ref.dtype)

    @pl.when(j >= GLU_TILES + GATE_TILES)
    def _():
        pa_ref[...] = _dot(xn_ref[...], w_ref[...]).astype(pa_ref.dtype)


def _in_proj(h, g, w_all, bias_all, layer):
    t, d = h.shape
    n_tiles = GLU_TILES + GATE_TILES + ATTN_TILES
    clamp = lambda j, lo, n: jnp.clip(j - lo, 0, n - 1)
    bm = IN_ROW_TILE
    return pl.pallas_call(
        _in_proj_kernel,
        grid=(t // bm, n_tiles),
        in_specs=[
            pl.BlockSpec((bm, d), lambda i, j: (i, 0)),
            pl.BlockSpec((1, d), lambda i, j: (0, 0)),
            pl.BlockSpec((None, d, IN_TILE), lambda i, j: (layer, 0, j)),
            pl.BlockSpec((None, 1, IN_TILE), lambda i, j: (layer, 0, j)),
        ],
        out_specs=[
            pl.BlockSpec((bm, IN_TILE // 2), lambda i, j: (i, clamp(j, 0, GLU_TILES))),
            pl.BlockSpec((bm, IN_TILE), lambda i, j: (i, clamp(j, GLU_TILES, GATE_TILES))),
            pl.BlockSpec((bm, IN_TILE), lambda i, j: (i, clamp(j, GLU_TILES + GATE_TILES, ATTN_TILES))),
        ],
        out_shape=[jax.ShapeDtypeStruct((t, C_CONV), BF16),
                   jax.ShapeDtypeStruct((t, 3 * D_MODEL), BF16),
                   jax.ShapeDtypeStruct((t, A_COLS), BF16)],
        scratch_shapes=[pltpu.VMEM((bm, d), BF16)],
        compiler_params=_cparams(2),
        name="in_proj",
    )(h, g.reshape(1, d), w_all, bias_all)


def _conv_stage(z_ref, zp_ref, seq):
    tail = zp_ref.shape[0] - CONV_PAD - seq
    zp_ref[0:CONV_PAD, :] = jnp.zeros((CONV_PAD, LANES), F32)
    zp_ref[CONV_PAD + seq:, :] = jnp.zeros((tail, LANES), F32)
    zp_ref[CONV_PAD:CONV_PAD + seq, :] = z_ref[...].astype(F32)


def _conv_chunk_thunks(zp_ref, dw_ref, cb_ref, o_ref, seq):
    def make(base):
        def thunk():
            acc = jnp.zeros((CONV_ROWS, LANES), F32) + cb_ref[...]
            for j in range(CONV_K):
                off = CONV_PAD - CONV_K // 2 + j
                acc = acc + zp_ref[base + off:base + off + CONV_ROWS, :] * dw_ref[j:j + 1, :]
            o_ref[base:base + CONV_ROWS, :] = acc.astype(o_ref.dtype)
            bits = pltpu.bitcast(acc[0:8, :], jnp.uint32)
            zero = lax.shift_right_logical(lax.shift_right_logical(bits, jnp.uint32(16)), jnp.uint32(16))
            return pltpu.bitcast(zero, F32)[0:1, :]
        return thunk
    return [make(c * CONV_ROWS) for c in range(seq // CONV_ROWS)]


def _softmax_pv(q, k_s, v_s, s_ref, p_ref, o_ref, n_real, n_meta, sub_tiles, side_work=()):
    lane = lax.broadcasted_iota(jnp.int32, (1, LANES), 1)
    tail_bias = jnp.where(lane < n_meta, 0.0, NEG_BIG).astype(F32)
    slot_rows = max(sub_tiles)
    starts = [sum(sub_tiles[:i]) for i in range(len(sub_tiles))]

    ordering = [jnp.zeros((1, LANES), F32)]

    def scores(idx):
        r0, rows, b0 = starts[idx], sub_tiles[idx], (idx % ATTN_RING) * slot_rows
        qs = q[r0:r0 + rows, :]
        s_ref[b0:b0 + rows, 0:n_real] = _dot_nt(qs, k_s[0:n_real, :])
        s_ref[b0:b0 + rows, n_real:n_real + LANES] = (
            _dot_nt(qs, k_s[n_real:n_real + LANES, :]) + (tail_bias + ordering[0]))

    def finish(idx):
        r0, rows, b0 = starts[idx], sub_tiles[idx], (idx % ATTN_RING) * slot_rows
        s = s_ref[b0:b0 + rows, :]
        m = jnp.max(s, axis=-1, keepdims=True)
        p_ref[b0:b0 + rows, :] = jnp.exp2(s - m).astype(BF16)
        o_ext = _dot(p_ref[b0:b0 + rows, :], v_s[...])
        o_ref[r0:r0 + rows, :] = (o_ext[:, 0:LANES] / o_ext[:, LANES:2 * LANES]).astype(o_ref.dtype)

    n_sub = len(sub_tiles)
    for idx in range(min(ATTN_LOOKAHEAD, n_sub)):
        scores(idx)
    groups = max(1, n_sub - ATTN_LOOKAHEAD - 1 - SIDE_SLACK)
    pending = {}
    for idx in range(n_sub):
        if idx in pending:
            ordering[0] = pending.pop(idx)
        if idx + ATTN_LOOKAHEAD < n_sub:
            scores(idx + ATTN_LOOKAHEAD)
        finish(idx)
        if idx < groups:
            zeros = [thunk() for thunk in
                     side_work[idx * len(side_work) // groups:(idx + 1) * len(side_work) // groups]]
            if zeros:
                pending[idx + 1 + SIDE_SLACK] = functools.reduce(lambda a, b: a + b, zeros)


def _fill_keys(dst_ref, col, make_rows, n_real, n_meta):
    dst_ref[0:n_real, col:col + LANES] = make_rows(n_meta, n_real)
    dst_ref[n_real:n_real + n_meta, col:col + LANES] = make_rows(0, n_meta)
    dst_ref[n_real + n_meta:n_real + LANES, col:col + LANES] = jnp.zeros((LANES - n_meta, LANES), BF16)


def _fill_ones(v_s):
    v_s[:, LANES:2 * LANES] = jnp.ones((v_s.shape[0], LANES), BF16)


def _gqa_kernel(q_ref, k_ref, v_ref, cos_ref, lo_ref, hi_ref, qg_ref, kg_ref, z_ref, dw_ref, cb_ref,
                o_ref, c_ref, k_s, v_s, s_ref, p_ref, zp_ref, *, n_real, n_meta):
    head = pl.program_id(1)
    shift = HEAD_DIM // 4

    @pl.when(lax.rem(head, GQA_GROUP) == 0)
    def _():
        def k_rows(start, size):
            x = _rms(k_ref[start:start + size, :].astype(F32), kg_ref[...])
            x = _rope(x, cos_ref[start:start + size, :], lo_ref[start:start + size, :],
                      hi_ref[start:start + size, :], shift)
            return x.astype(BF16)

        def v_rows(start, size):
            return v_ref[start:start + size, :]

        _fill_keys(k_s, 0, k_rows, n_real, n_meta)
        _fill_keys(v_s, 0, v_rows, n_real, n_meta)
        _fill_ones(v_s)

    q = _rms(q_ref[...].astype(F32), qg_ref[...])
    q = _rope(q, cos_ref[...], lo_ref[...], hi_ref[...], shift)
    q = (q * (LOG2E / math.sqrt(HEAD_DIM))).astype(BF16)
    seq = n_real + n_meta
    _conv_stage(z_ref, zp_ref, seq)
    _softmax_pv(q, k_s, v_s, s_ref, p_ref, o_ref, n_real, n_meta, GQA_SUB_TILES,
                side_work=_conv_chunk_thunks(zp_ref, dw_ref, cb_ref, c_ref, seq))


def _gqa_attention(pa, tabs, qn_g, kn_g, z, dw, cb, batch, seq):
    assert GQA_HEADS == C_CONV // LANES
    z3 = z.reshape(batch, seq, C_CONV)
    pa3 = pa.reshape(batch, seq, A_COLS)
    n_real = seq - N_META
    q_blk, k_blk, v_blk = A_Q // LANES, A_K // LANES, A_V // LANES
    tab = pl.BlockSpec((seq, LANES), lambda b, h: (0, 0))
    gspec = pl.BlockSpec((1, LANES), lambda b, h: (0, 0))
    out, conv = pl.pallas_call(
        functools.partial(_gqa_kernel, n_real=n_real, n_meta=N_META),
        grid=(batch, GQA_HEADS),
        in_specs=[
            pl.BlockSpec((None, seq, LANES), lambda b, h: (b, 0, q_blk + h)),
            pl.BlockSpec((None, seq, LANES), lambda b, h: (b, 0, k_blk + h // GQA_GROUP)),
            pl.BlockSpec((None, seq, LANES), lambda b, h: (b, 0, v_blk + h // GQA_GROUP)),
            tab, tab, tab, gspec, gspec,
            pl.BlockSpec((None, seq, LANES), lambda b, h: (b, 0, h)),
            pl.BlockSpec((CONV_K, LANES), lambda b, h: (0, h)),
            pl.BlockSpec((1, LANES), lambda b, h: (0, h)),
        ],
        out_specs=[pl.BlockSpec((None, seq, LANES), lambda b, h: (b, 0, h)),
                   pl.BlockSpec((None, seq, LANES), lambda b, h: (b, 0, h))],
        out_shape=[jax.ShapeDtypeStruct((batch, seq, GQA_HEADS * HEAD_DIM), BF16),
                   jax.ShapeDtypeStruct((batch, seq, C_CONV), BF16)],
        scratch_shapes=[pltpu.VMEM((n_real + LANES, HEAD_DIM), BF16),
                        pltpu.VMEM((n_real + LANES, 2 * LANES), BF16),
                        pltpu.VMEM((ATTN_RING * max(GQA_SUB_TILES), n_real + LANES), F32),
                        pltpu.VMEM((ATTN_RING * max(GQA_SUB_TILES), n_real + LANES), BF16),
                        pltpu.VMEM((CONV_PAD + seq + CONV_PAD, LANES), F32)],
        compiler_params=_cparams(2),
        name="gqa_attention",
    )(pa3, pa3, pa3, tabs[0], tabs[1], tabs[2],
      qn_g.reshape(1, HEAD_DIM), kn_g.reshape(1, HEAD_DIM), z3, dw, cb.reshape(1, C_CONV))
    return out.reshape(batch * seq, GQA_HEADS * HEAD_DIM), conv.reshape(batch * seq, C_CONV)


def _mla_kernel(q_ref, kv_ref, kpe_ref, cos_ref, lo_ref, hi_ref, o_ref, k_s, v_s, s_ref, p_ref, *, n_real, n_meta):
    shift = MLA_ROPE // 4

    def knope_rows(start, size):
        return kv_ref[start:start + size, 0:LANES]

    def kpe_rows(start, size):
        x = _rope(kpe_ref[start:start + size, :].astype(F32), cos_ref[start:start + size, :],
                  lo_ref[start:start + size, :], hi_ref[start:start + size, :], shift)
        return x.astype(BF16)

    def v_rows(start, size):
        return kv_ref[start:start + size, LANES:2 * LANES]

    @pl.when(pl.program_id(1) == 0)
    def _():
        _fill_keys(k_s, LANES, kpe_rows, n_real, n_meta)
        _fill_ones(v_s)

    _fill_keys(k_s, 0, knope_rows, n_real, n_meta)
    _fill_keys(v_s, 0, v_rows, n_real, n_meta)

    scale = LOG2E / math.sqrt(MLA_QK)
    q_nope = q_ref[:, 0:LANES].astype(F32) * scale
    q_pe = _rope(q_ref[:, LANES:2 * LANES].astype(F32), cos_ref[...], lo_ref[...], hi_ref[...], shift) * scale
    q = jnp.concatenate([q_nope, q_pe], axis=-1).astype(BF16)
    _softmax_pv(q, k_s, v_s, s_ref, p_ref, o_ref, n_real, n_meta, MLA_SUB_TILES)


def _mla_attention(q_mla, kv_mla, pa, tabs, batch, seq):
    n_real = seq - N_META
    width = 2 * LANES
    q3 = q_mla.reshape(batch, seq, MLA_HEADS * width)
    kv3 = kv_mla.reshape(batch, seq, MLA_HEADS * width)
    pa3 = pa.reshape(batch, seq, A_COLS)
    kpe_blk = A_KPE // LANES
    tab = pl.BlockSpec((seq, LANES), lambda b, h: (0, 0))
    out = pl.pallas_call(
        functools.partial(_mla_kernel, n_real=n_real, n_meta=N_META),
        grid=(batch, MLA_HEADS),
        in_specs=[
            pl.BlockSpec((None, seq, width), lambda b, h: (b, 0, h)),
            pl.BlockSpec((None, seq, width), lambda b, h: (b, 0, h)),
            pl.BlockSpec((None, seq, LANES), lambda b, h: (b, 0, kpe_blk)),
            tab, tab, tab,
        ],
        out_specs=pl.BlockSpec((None, seq, LANES), lambda b, h: (b, 0, h)),
        out_shape=jax.ShapeDtypeStruct((batch, seq, MLA_HEADS * LANES), BF16),
        scratch_shapes=[pltpu.VMEM((n_real + LANES, width), BF16),
                        pltpu.VMEM((n_real + LANES, 2 * LANES), BF16),
                        pltpu.VMEM((ATTN_RING * max(MLA_SUB_TILES), n_real + LANES), F32),
                        pltpu.VMEM((ATTN_RING * max(MLA_SUB_TILES), n_real + LANES), BF16)],
        compiler_params=_cparams(2),
        name="mla_attention",
    )(q3, kv3, pa3, tabs[0], tabs[1], tabs[2])
    return out.reshape(batch * seq, MLA_HEADS * LANES)


def _merge_kernel(c_ref, ob_ref, oc_ref, g0_ref, g1_ref, g2_ref, h_ref, lng_ref, lnb_ref,
                  wa_ref, wb_ref, wc_ref, wo_ref, o_ref):
    c = c_ref[...].astype(F32)
    mu = jnp.mean(c, axis=-1, keepdims=True)
    xc = c - mu
    y = xc * lax.rsqrt(jnp.mean(xc * xc, axis=-1, keepdims=True) + NORM_EPS)
    y = y * lng_ref[...] + lnb_ref[...]
    act = (y * jax.nn.sigmoid(y)).astype(BF16)
    merged = g0_ref[...].astype(F32) * _dot(act, wa_ref[...])
    merged = merged + g1_ref[...].astype(F32) * _dot(ob_ref[...], wb_ref[...])
    merged = merged + g2_ref[...].astype(F32) * _dot(oc_ref[...], wc_ref[...])
    o_ref[...] = h_ref[...] + _dot(merged.astype(BF16), wo_ref[...])


def _merge(c, o_b, o_c, gates, h, ln_g, ln_b, w_a, w_b, w_c, w_o, layer):
    t, d = h.shape
    bm = MERGE_ROW_TILE
    row = lambda width, col: pl.BlockSpec((bm, width), lambda i: (i, col))
    const = lambda shape: pl.BlockSpec(shape, lambda i: (0, 0), pipeline_mode=pl.Buffered(1))
    weight = lambda k: pl.BlockSpec((None, k, d), lambda i: (layer, 0, 0), pipeline_mode=pl.Buffered(1))
    return pl.pallas_call(
        _merge_kernel,
        grid=(t // bm,),
        in_specs=[row(C_CONV, 0), row(C_CONV, 0), row(C_CONV, 0),
                  row(d, 0), row(d, 1), row(d, 2), row(d, 0),
                  const((1, C_CONV)), const((1, C_CONV)),
                  weight(C_CONV), weight(C_CONV), weight(C_CONV), weight(d)],
        out_specs=row(d, 0),
        out_shape=jax.ShapeDtypeStruct((t, d), F32),
        compiler_params=_cparams(1),
        name="branch_merge",
    )(c, o_b, o_c, gates, gates, gates, h, ln_g.reshape(1, C_CONV), ln_b.reshape(1, C_CONV),
      w_a, w_b, w_c, w_o)


def _ffn_kernel(h_ref, g_ref, wg_ref, wu_ref, wd_ref, o_ref, v_ref):
    def step(v, base):
        gate = _dot(v, wg_ref[...])
        up = _dot(v, wu_ref[...])
        a = (gate * jax.nn.sigmoid(gate) * up).astype(BF16)
        o_ref[...] = base + _dot(a, wd_ref[...])

    @pl.when(pl.program_id(1) == 0)
    def _():
        hv = h_ref[...]
        v = _rms(hv, g_ref[...]).astype(BF16)
        v_ref[...] = v
        step(v, hv)

    @pl.when(pl.program_id(1) > 0)
    def _():
        step(v_ref[...], o_ref[...])


def _ffn(h, g, w_gate, w_up, w_down, layer):
    t, d = h.shape
    return pl.pallas_call(
        _ffn_kernel,
        grid=(t // ROW_TILE, D_FF // FF_TILE),
        in_specs=[
            pl.BlockSpec((ROW_TILE, d), lambda i, f: (i, 0)),
            pl.BlockSpec((1, d), lambda i, f: (0, 0)),
            pl.BlockSpec((None, d, FF_TILE), lambda i, f: (layer, 0, f)),
            pl.BlockSpec((None, d, FF_TILE), lambda i, f: (layer, 0, f)),
            pl.BlockSpec((None, FF_TILE, d), lambda i, f: (layer, f, 0)),
        ],
        out_specs=pl.BlockSpec((ROW_TILE, d), lambda i, f: (i, 0)),
        out_shape=jax.ShapeDtypeStruct((t, d), F32),
        scratch_shapes=[pltpu.VMEM((ROW_TILE, d), BF16)],
        compiler_params=_cparams(2),
        name="swiglu_ffn",
    )(h, g.reshape(1, d), w_gate, w_up, w_down)


def _final_norm_kernel(h_ref, g_ref, o_ref):
    o_ref[...] = _rms(h_ref[...], g_ref[...])


def _final_norm(h, g, batch, seq):
    d = h.shape[1]
    n_tok = seq - N_META
    return pl.pallas_call(
        _final_norm_kernel,
        grid=(batch, n_tok // FINAL_ROWS),
        in_specs=[pl.BlockSpec((pl.Element(FINAL_ROWS), pl.Element(d)),
                               lambda b, r: ((b * (seq // 8) + N_META // 8 + r * (FINAL_ROWS // 8)) * 8, 0)),
                  pl.BlockSpec((1, d), lambda b, r: (0, 0))],
        out_specs=pl.BlockSpec((None, FINAL_ROWS, d), lambda b, r: (b, r, 0)),
        out_shape=jax.ShapeDtypeStruct((batch, n_tok, d), F32),
        compiler_params=_cparams(2),
        name="final_rmsnorm",
    )(h, g.reshape(1, d))


def _rope_tables(n_tok, rot_dim):
    rows = n_tok // GRID_W
    row = jnp.concatenate([jnp.zeros((N_META,), F32), jnp.repeat(jnp.arange(rows, dtype=F32), GRID_W)])
    col = jnp.concatenate([jnp.zeros((N_META,), F32), jnp.tile(jnp.arange(GRID_W, dtype=F32), rows)])
    sub = rot_dim // 4
    inv = ROPE_THETA ** (-jnp.arange(sub, dtype=F32) / sub)
    ang_r = row[:, None] * inv[None, :]
    ang_c = col[:, None] * inv[None, :]
    cr, sr, cc, sc = jnp.cos(ang_r), jnp.sin(ang_r), jnp.cos(ang_c), jnp.sin(ang_c)
    zero = jnp.zeros_like(sr)
    l = row.shape[0]
    pad1 = jnp.ones((l, LANES - rot_dim), F32)
    pad0 = jnp.zeros((l, LANES - rot_dim), F32)
    cos = jnp.concatenate([cr, cr, cc, cc, pad1], axis=1)
    sin_lo = jnp.concatenate([-sr, zero, -sc, zero, pad0], axis=1)
    sin_hi = jnp.concatenate([zero, sr, zero, sc, pad0], axis=1)
    return cos, sin_lo, sin_hi


def _in_proj_segments():
    src = {}
    o = 0
    for name, width in (("a", C_CONV), ("gte", C_CONV), ("q", 1024), ("k", 256), ("v", 256),
                        ("cq", MLA_RANK), ("ckv", MLA_RANK), ("kpe", MLA_ROPE), ("gates", 3 * D_MODEL)):
        src[name] = o
        o += width
    half = IN_TILE // 2
    segs = []
    for t in range(GLU_TILES):
        segs.append((src["a"] + t * half, t * IN_TILE, half))
        segs.append((src["gte"] + t * half, t * IN_TILE + half, half))
    base = GLU_TILES * IN_TILE
    segs.append((src["gates"], base, 3 * D_MODEL))
    base += 3 * D_MODEL
    for name, off, width in (("q", A_Q, 1024), ("cq", A_CQ, MLA_RANK), ("ckv", A_CKV, MLA_RANK),
                             ("k", A_K, 256), ("v", A_V, 256), ("kpe", A_KPE, MLA_ROPE)):
        segs.append((src[name], base + off, width))
    return segs, base + A_KPE + MLA_ROPE, base + A_COLS


def _pack_sources():
    raw, _, total = _in_proj_segments()
    segs = []
    for s, d, w in sorted(raw, key=lambda seg: seg[1]):
        if segs and segs[-1][0] + segs[-1][2] == s and segs[-1][1] + segs[-1][2] == d:
            segs[-1] = (segs[-1][0], segs[-1][1], segs[-1][2] + w)
        else:
            segs.append((s, d, w))
    table = []
    for dst in range(0, total, PACK_COLS):
        hit = [(s + dst - d, min(w - (dst - d), PACK_COLS)) for s, d, w in segs if d <= dst < d + w]
        table.append(hit[0] if hit else (0, 0))
    return table


def _pack_w_in_kernel(x_ref, o_ref):
    blk = pl.program_id(1)
    valid = functools.reduce(lambda acc, kv: jnp.where(blk == kv[0], kv[1][1], acc),
                             enumerate(_pack_sources()), 0)
    row = lax.broadcasted_iota(jnp.int32, (PACK_COLS, 1), 0)
    x = jnp.where(row < valid, x_ref[0], 0.0)
    o_ref[...] = x.T.astype(o_ref.dtype)


def _pack_w_in(w_in):
    depth, d, n_in = w_in.shape
    table = _pack_sources()
    assert all(s % 8 == 0 and s + PACK_COLS <= n_in for s, _ in table)

    def src_row(blk):
        off8 = functools.reduce(lambda acc, kv: jnp.where(blk == kv[0], kv[1][0] // 8, acc),
                                enumerate(table), 0)
        return off8 * 8

    w_t = jnp.swapaxes(w_in, 1, 2)
    return pl.pallas_call(
        _pack_w_in_kernel,
        grid=(depth, len(table)),
        in_specs=[pl.BlockSpec((pl.Element(1), pl.Element(PACK_COLS), pl.Element(d)),
                               lambda a, blk: (a, src_row(blk), 0))],
        out_specs=pl.BlockSpec((None, d, PACK_COLS), lambda a, blk: (a, 0, blk)),
        out_shape=jax.ShapeDtypeStruct((depth, d, len(table) * PACK_COLS), BF16),
        compiler_params=_cparams(2),
        name="pack_w_in",
    )(w_t)


def _cast_kernel(x_ref, o_ref):
    o_ref[...] = x_ref[...].astype(o_ref.dtype)


def _to_bf16(w):
    depth, k, n = w.shape
    rows = min(k, PACK_ROWS)
    return pl.pallas_call(
        _cast_kernel,
        grid=(depth, k // rows),
        in_specs=[pl.BlockSpec((None, rows, n), lambda a, r: (a, r, 0))],
        out_specs=pl.BlockSpec((None, rows, n), lambda a, r: (a, r, 0)),
        out_shape=jax.ShapeDtypeStruct(w.shape, BF16),
        compiler_params=_cparams(2),
        name="cast_bf16",
    )(w)


def _pad_uq(w_uq):
    depth = w_uq.shape[0]
    wq = w_uq.reshape(depth, MLA_RANK, MLA_HEADS, MLA_QK)
    wq = jnp.concatenate([wq, jnp.zeros((depth, MLA_RANK, MLA_HEADS, 2 * LANES - MLA_QK), w_uq.dtype)], axis=3)
    return wq.reshape(depth, MLA_RANK, MLA_HEADS * 2 * LANES).astype(BF16)


def _in_proj_bias(gate_b):
    depth = gate_b.shape[0]
    n_glu = GLU_TILES * IN_TILE
    return jnp.concatenate([jnp.zeros((depth, 1, n_glu), F32), gate_b.astype(F32)[:, None, :],
                            jnp.zeros((depth, 1, A_COLS), F32)], axis=2)


def kernel(x, meta_tokens, mix_norm_g, w_in, conv_dw, conv_b, conv_ln_g, conv_ln_b, w_conv_out, gqa_q_norm_g, gqa_k_norm_g, w_gqa_out, mla_q_norm_g, w_mla_uq, mla_kv_norm_g, w_mla_ukv, w_mla_out, gate_b, w_out, ffn_norm_g, w_ffn_gate, w_ffn_up, w_ffn_down, final_norm_g):
    batch, n_tok, d = x.shape
    seq = N_META + n_tok
    depth = w_in.shape[0]
    assert d == D_MODEL and seq == Q_TILE and seq % CONV_ROWS == 0
    assert sum(GQA_SUB_TILES) == Q_TILE and sum(MLA_SUB_TILES) == Q_TILE
    assert all((batch * seq) % tile == 0 for tile in (ROW_TILE, IN_ROW_TILE, MERGE_ROW_TILE))

    meta = jnp.broadcast_to(meta_tokens.astype(x.dtype)[None], (batch, N_META, d))
    h = jnp.concatenate([meta, x], axis=1).reshape(batch * seq, d)

    tabs_gqa = _rope_tables(n_tok, HEAD_DIM)
    tabs_mla = _rope_tables(n_tok, MLA_ROPE)

    w_all = _pack_w_in(w_in)
    bias_all = _in_proj_bias(gate_b)
    w_uq = _pad_uq(w_mla_uq)
    w_ukv, w_a, w_b, w_c, w_o = (_to_bf16(w) for w in (w_mla_ukv, w_conv_out, w_gqa_out, w_mla_out, w_out))
    w_fg, w_fu, w_fd = (_to_bf16(w) for w in (w_ffn_gate, w_ffn_up, w_ffn_down))

    for i in range(depth):
        z, gates, pa = _in_proj(h, mix_norm_g[i], w_all, bias_all, i)

        o_b, c = _gqa_attention(pa, tabs_gqa, gqa_q_norm_g[i], gqa_k_norm_g[i],
                                z, conv_dw[i].reshape(CONV_K, C_CONV), conv_b[i], batch, seq)

        q_mla, kv_mla = _mla_up(pa, mla_q_norm_g[i], mla_kv_norm_g[i], w_uq, w_ukv, i)
        o_c = _mla_attention(q_mla, kv_mla, pa, tabs_mla, batch, seq)

        h = _merge(c, o_b, o_c, gates, h, conv_ln_g[i], conv_ln_b[i], w_a, w_b, w_c, w_o, i)

        h = _ffn(h, ffn_norm_g[i], w_fg, w_fu, w_fd, i)

    return _final_norm(h, final_norm_g, batch, seq)
```
